```python
import jax, jax.numpy as jnp
from jax import lax
import numpy as np

D_MODEL = 1024
BATCH = 1
SEQ = 16384
DEPTH = 2

N_META = 16
N_A_LAYERS = DEPTH // 2
N_B_LAYERS = DEPTH - N_A_LAYERS
CONV_W = 3
N_HEADS = 8
QK_NOPE_DIM = 128
QK_ROPE_DIM = 64
V_DIM = 128
Q_RANK = 384
KV_RANK = 256
ROPE_THETA = 10000.0
Q_BLOCK = 128
N_GROUPS = 8
EXPERTS_PER_GROUP = 8
N_EXPERTS = N_GROUPS * EXPERTS_PER_GROUP
TOP_K_IN_GROUP = 2
D_EXPERT = 256
EXPERT_BLOCK = 128
NORM_EPS = 1e-6

kernel_name = 'yoco_shortconv_mla_hier_moe'


def rmsnorm(x, g):
    xf = x.astype(jnp.float32)
    y = xf * lax.rsqrt(jnp.mean(xf * xf, axis=-1, keepdims=True) + NORM_EPS)
    return (y * g.astype(jnp.float32)).astype(x.dtype)


def rope_tables(t):
    inv = ROPE_THETA ** (-jnp.arange(0, QK_ROPE_DIM, 2, dtype=jnp.float32) / QK_ROPE_DIM)
    ang = jnp.arange(t, dtype=jnp.float32)[:, None] * inv[None, :]
    return jnp.cos(ang), jnp.sin(ang)


def apply_rope(x, cos, sin):
    half = x.shape[-1] // 2
    x1, x2 = x[..., :half], x[..., half:]
    cos = cos.astype(x.dtype)
    sin = sin.astype(x.dtype)
    return jnp.concatenate([x1 * cos - x2 * sin, x1 * sin + x2 * cos], axis=-1)


def short_conv_mixer(hn, w_in, conv_w, w_out):
    bcu = hn @ w_in
    b, c, u = jnp.split(bcu, 3, axis=-1)
    z = c * u
    z = lax.conv_general_dilated(z, conv_w.astype(z.dtype), window_strides=(1,),
                                 padding=((CONV_W - 1, 0),),
                                 dimension_numbers=('NWC', 'WIO', 'NWC'),
                                 feature_group_count=z.shape[-1])
    return (b * z) @ w_out


def shared_kv_side(h, kv_g, w_dkv, ckv_g, w_ukv, cos, sin):
    bsz, t, _ = h.shape
    a = rmsnorm(h, kv_g) @ w_dkv
    c_kv = rmsnorm(a[..., :KV_RANK], ckv_g)
    k_rope = apply_rope(a[..., KV_RANK:], cos, sin)
    kv = (c_kv @ w_ukv).reshape(bsz, t, N_HEADS, QK_NOPE_DIM + V_DIM)
    return kv[..., :QK_NOPE_DIM], k_rope, kv[..., QK_NOPE_DIM:]


def causal_block_attention(q_nope, q_rope, k_nope, k_rope, v):
    bsz, t, nh, _ = q_nope.shape
    nqb = -(-t // Q_BLOCK)
    pad = nqb * Q_BLOCK - t
    def blocks(q):
        q = jnp.pad(q, ((0, 0), (0, pad), (0, 0), (0, 0)))
        return q.reshape(bsz, nqb, Q_BLOCK, nh, q.shape[-1]).transpose(1, 0, 2, 3, 4)
    qn, qr = blocks(q_nope), blocks(q_rope)
    kpos = jnp.arange(t)
    scale = (QK_NOPE_DIM + QK_ROPE_DIM) ** -0.5
    def one_block(args):
        qn_b, qr_b, j = args
        qpos = j * Q_BLOCK + jnp.arange(Q_BLOCK)
        s = (jnp.einsum('bqhd,bkhd->bhqk', qn_b, k_nope)
             + jnp.einsum('bqhd,bkd->bhqk', qr_b, k_rope)).astype(jnp.float32) * scale
        s = jnp.where(kpos[None, :] <= qpos[:, None], s, -jnp.inf)
        p = jax.nn.softmax(s, axis=-1).astype(v.dtype)
        return jnp.einsum('bhqk,bkhd->bqhd', p, v)
    o = lax.map(one_block, (qn, qr, jnp.arange(nqb, dtype=jnp.int32)))
    o = o.transpose(1, 0, 2, 3, 4).reshape(bsz, nqb * Q_BLOCK, nh * v.shape[-1])
    return o[:, :t]


def mla_mixer(hn, w_dq, cq_g, w_uq, w_o, k_nope, k_rope, v, cos, sin):
    bsz, t, _ = hn.shape
    cq = rmsnorm(hn @ w_dq, cq_g)
    q = (cq @ w_uq).reshape(bsz, t, N_HEADS, QK_NOPE_DIM + QK_ROPE_DIM)
    q_nope = q[..., :QK_NOPE_DIM]
    q_rope = apply_rope(q[..., QK_NOPE_DIM:], cos[:, None, :], sin[:, None, :])
    o = causal_block_attention(q_nope, q_rope, k_nope, k_rope, v)
    return o @ w_o


def hier_moe(hn, w_grp, w_exp, w_gate, w_up, w_down):
    bsz, t, d = hn.shape
    n = bsz * t
    xf = hn.reshape(n, d)
    g_logits = (xf @ w_grp).astype(jnp.float32)
    p_grp = jax.nn.softmax(g_logits, axis=-1)
    grp = jnp.argmax(g_logits, axis=-1).astype(jnp.int32)
    p_g = jnp.max(p_grp, axis=-1, keepdims=True)
    e_logits = (xf @ w_exp).astype(jnp.float32).reshape(n, N_GROUPS, EXPERTS_PER_GROUP)
    e_in = jnp.einsum('ng,nge->ne', jax.nn.one_hot(grp, N_GROUPS, dtype=jnp.float32), e_logits)
    top_p, top_i = lax.top_k(jax.nn.softmax(e_in, axis=-1), TOP_K_IN_GROUP)
    gate = p_g * top_p / jnp.sum(top_p, axis=-1, keepdims=True)
    expert = grp[:, None] * EXPERTS_PER_GROUP + top_i.astype(jnp.int32)
    a = n * TOP_K_IN_GROUP
    e_a = expert.reshape(a)
    w_a = gate.reshape(a).astype(hn.dtype)
    tok_a = jnp.repeat(jnp.arange(n, dtype=jnp.int32), TOP_K_IN_GROUP)
    counts = jax.ops.segment_sum(jnp.ones((a,), jnp.int32), e_a, num_segments=N_EXPERTS)
    padded = ((counts + EXPERT_BLOCK - 1) // EXPERT_BLOCK) * EXPERT_BLOCK
    pad_end = jnp.cumsum(padded)
    pad_start = pad_end - padded
    start = jnp.cumsum(counts) - counts
    order = jnp.argsort(e_a)
    e_s = e_a[order]
    dest = pad_start[e_s] + jnp.arange(a, dtype=jnp.int32) - start[e_s]
    n_blocks = (a + N_EXPERTS * (EXPERT_BLOCK - 1) + EXPERT_BLOCK - 1) // EXPERT_BLOCK
    p_slots = n_blocks * EXPERT_BLOCK
    slot_tok = jnp.full((p_slots,), n, jnp.int32).at[dest].set(tok_a[order])
    slot_w = jnp.zeros((p_slots,), hn.dtype).at[dest].set(w_a[order])
    blk_expert = jnp.minimum(jnp.searchsorted(pad_end, jnp.arange(n_blocks, dtype=jnp.int32) * EXPERT_BLOCK,
                                              side='right'), N_EXPERTS - 1).astype(jnp.int32)
    x_pad = jnp.concatenate([xf, jnp.zeros((1, d), xf.dtype)], axis=0)
    xs = x_pad[slot_tok].reshape(n_blocks, EXPERT_BLOCK, d)
    def expert_block(args):
        xb, e = args
        hh = jax.nn.silu(xb @ w_gate[e]) * (xb @ w_up[e])
        return hh @ w_down[e]
    ys = lax.map(expert_block, (xs, blk_expert)).reshape(p_slots, d)
    y = jnp.zeros((n + 1, d), ys.dtype).at[slot_tok].add(ys * slot_w[:, None])[:n]
    return y.reshape(bsz, t, d)


def setup_inputs(seed: int = 0) -> dict:
    key = jax.random.key(seed)
    ks = jax.random.split(key, 21)
    def nrm(k, shape, scale):
        return jax.random.normal(k, shape, jnp.float32) * scale
    d = D_MODEL
    return {
        'x': nrm(ks[0], (BATCH, SEQ, d), 1.0),
        'meta_tokens': nrm(ks[1], (N_META, d), 1.0),
        'norm_mix_g': 1.0 + nrm(ks[2], (DEPTH, d), 0.02),
        'norm_ffn_g': 1.0 + nrm(ks[3], (DEPTH, d), 0.02),
        'conv_w_in': nrm(ks[4], (N_A_LAYERS, d, 3 * d), d ** -0.5),
        'conv_w': nrm(ks[5], (N_A_LAYERS, CONV_W, 1, d), CONV_W ** -0.5),
        'conv_w_out': nrm(ks[6], (N_A_LAYERS, d, d), d ** -0.5),
        'kv_norm_g': 1.0 + nrm(ks[7], (d,), 0.02),
        'w_dkv': nrm(ks[8], (d, KV_RANK + QK_ROPE_DIM), d ** -0.5),
        'ckv_norm_g': 1.0 + nrm(ks[9], (KV_RANK,), 0.02),
        'w_ukv': nrm(ks[10], (KV_RANK, N_HEADS * (QK_NOPE_DIM + V_DIM)), KV_RANK ** -0.5),
        'w_dq': nrm(ks[11], (N_B_LAYERS, d, Q_RANK), d ** -0.5),
        'cq_norm_g': 1.0 + nrm(ks[12], (N_B_LAYERS, Q_RANK), 0.02),
        'w_uq': nrm(ks[13], (N_B_LAYERS, Q_RANK, N_HEADS * (QK_NOPE_DIM + QK_ROPE_DIM)), Q_RANK ** -0.5),
        'w_o': nrm(ks[14], (N_B_LAYERS, N_HEADS * V_DIM, d), (N_HEADS * V_DIM) ** -0.5),
        'w_grp': nrm(ks[15], (DEPTH, d, N_GROUPS), d ** -0.5),
        'w_exp': nrm(ks[16], (DEPTH, d, N_EXPERTS), d ** -0.5),
        'w_gate': nrm(ks[17], (DEPTH, N_EXPERTS, d, D_EXPERT), d ** -0.5),
        'w_up': nrm(ks[18], (DEPTH, N_EXPERTS, d, D_EXPERT), d ** -0.5),
        'w_down': nrm(ks[19], (DEPTH, N_EXPERTS, D_EXPERT, d), D_EXPERT ** -0.5),
        'final_norm_g': 1.0 + nrm(ks[20], (d,), 0.02),
    }


def reference(x, meta_tokens, norm_mix_g, norm_ffn_g, conv_w_in, conv_w, conv_w_out,
              kv_norm_g, w_dkv, ckv_norm_g, w_ukv, w_dq, cq_norm_g, w_uq, w_o,
              w_grp, w_exp, w_gate, w_up, w_down, final_norm_g):
    bsz = x.shape[0]
    meta = jnp.broadcast_to(meta_tokens[None].astype(x.dtype), (bsz, N_META, x.shape[-1]))
    h = jnp.concatenate([meta, x], axis=1)
    cos, sin = rope_tables(h.shape[1])
    shared = None
    for i in range(DEPTH):
        if i < N_A_LAYERS:
            h = h + short_conv_mixer(rmsnorm(h, norm_mix_g[i]), conv_w_in[i], conv_w[i], conv_w_out[i])
        else:
            if shared is None:
                shared = shared_kv_side(h, kv_norm_g, w_dkv, ckv_norm_g, w_ukv, cos, sin)
            j = i - N_A_LAYERS
            k_nope, k_rope, v = shared
            h = h + mla_mixer(rmsnorm(h, norm_mix_g[i]), w_dq[j], cq_norm_g[j], w_uq[j], w_o[j],
                              k_nope, k_rope, v, cos, sin)
        h = h + hier_moe(rmsnorm(h, norm_ffn_g[i]), w_grp[i], w_exp[i], w_gate[i], w_up[i], w_down[i])
    return rmsnorm(h, final_norm_g)[:, N_META:]
```

```python
import functools

import jax
import jax.numpy as jnp
from jax import lax
from jax.experimental import pallas as pl
from jax.experimental.pallas import tpu as pltpu

F32 = jnp.float32
BF16 = jnp.bfloat16

N_META = 16
N_HEADS = 8
QK_NOPE_DIM = 128
QK_ROPE_DIM = 64
V_DIM = 128
KV_RANK = 256
ROPE_THETA = 10000.0
N_GROUPS = 8
EXPERTS_PER_GROUP = 8
N_EXPERTS = N_GROUPS * EXPERTS_PER_GROUP
NORM_EPS = 1e-6

LANES = 128
HEAD_PAD = 256
TOK_TILE = 256
FFN_BLOCK = 256
EXPERT_LANE0 = 64
DMA_WINDOW = 32
VMEM_LIMIT = 48 * 1024 * 1024


def _rms(x, g):
    return x * lax.rsqrt(jnp.mean(x * x, axis=-1, keepdims=True) + NORM_EPS) * g


def _cparams(*sem):
    return pltpu.CompilerParams(dimension_semantics=sem, vmem_limit_bytes=VMEM_LIMIT)


def _full(shape):
    return pl.BlockSpec(shape, lambda *_: (0,) * len(shape))


def _conv_mixer_kernel(x_ref, g_ref, win_ref, cw_ref, wout_ref, o_ref, carry_ref):
    d = x_ref.shape[1]
    tm = x_ref.shape[0]

    @pl.when(pl.program_id(0) == 0)
    def _():
        carry_ref[...] = jnp.zeros_like(carry_ref)

    x = x_ref[...]
    hn = _rms(x, g_ref[...]).astype(BF16)
    bcu = jnp.dot(hn, win_ref[...], preferred_element_type=F32)
    b, c, u = bcu[:, :d], bcu[:, d:2 * d], bcu[:, 2 * d:]
    z = c * u
    row = lax.broadcasted_iota(jnp.int32, z.shape, 0)
    prev1 = carry_ref[7:8, :]
    prev2 = carry_ref[6:7, :]
    z1 = jnp.where(row == 0, prev1, pltpu.roll(z, 1, axis=0))
    z2 = jnp.where(row == 0, prev2, jnp.where(row == 1, prev1, pltpu.roll(z, 2, axis=0)))
    conv = cw_ref[0:1, :] * z2 + cw_ref[1:2, :] * z1 + cw_ref[2:3, :] * z
    carry_ref[...] = z[tm - 8:, :]
    y = (b * conv).astype(BF16)
    o_ref[...] = x + jnp.dot(y, wout_ref[...], preferred_element_type=F32)


def _conv_mixer(h, g, w_in, conv_w, w_out):
    tp, d = h.shape
    tm = TOK_TILE
    return pl.pallas_call(
        _conv_mixer_kernel,
        grid=(tp // tm,),
        in_specs=[
            pl.BlockSpec((tm, d), lambda i: (i, 0)),
            _full((1, d)),
            _full((d, 3 * d)),
            _full((3, d)),
            _full((d, d)),
        ],
        out_specs=pl.BlockSpec((tm, d), lambda i: (i, 0)),
        out_shape=jax.ShapeDtypeStruct((tp, d), F32),
        scratch_shapes=[pltpu.VMEM((8, d), F32)],
        compiler_params=_cparams("arbitrary"),
        name="conv_mixer",
    )(h, g, w_in, conv_w, w_out)


def _router_kernel(h_ref, g_ref, wr_ref, hn_ref, info_ref, gate_ref, cnt_ref, carry_ref):
    tm = h_ref.shape[0]

    @pl.when(pl.program_id(0) == 0)
    def _():
        carry_ref[...] = jnp.zeros_like(carry_ref)

    hn = _rms(h_ref[...], g_ref[...])
    hn_ref[...] = hn
    logits = jnp.dot(hn, wr_ref[...], precision=lax.Precision.HIGHEST,
                     preferred_element_type=F32)
    lane = lax.broadcasted_iota(jnp.int32, logits.shape, 1).astype(F32)
    neg = jnp.float32(-jnp.inf)
    big = jnp.float32(2 * LANES)

    is_g = lane < N_GROUPS
    gmax = jnp.max(jnp.where(is_g, logits, neg), axis=1, keepdims=True)
    grp = jnp.min(jnp.where(is_g & (logits == gmax), lane, big), axis=1, keepdims=True)
    gsum = jnp.sum(jnp.where(is_g, jnp.exp(logits - gmax), 0.0), axis=1, keepdims=True)
    p_g = 1.0 / gsum

    lo = EXPERT_LANE0 + EXPERTS_PER_GROUP * grp
    in_grp = (lane >= lo) & (lane < lo + EXPERTS_PER_GROUP)
    emax = jnp.max(jnp.where(in_grp, logits, neg), axis=1, keepdims=True)
    ex = jnp.where(in_grp, jnp.exp(logits - emax), 0.0)
    p = ex / jnp.sum(ex, axis=1, keepdims=True)
    pm1 = jnp.where(in_grp, p, -1.0)
    p1 = jnp.max(pm1, axis=1, keepdims=True)
    i1 = jnp.min(jnp.where(pm1 == p1, lane, big), axis=1, keepdims=True)
    pm2 = jnp.where(lane == i1, -1.0, pm1)
    p2 = jnp.max(pm2, axis=1, keepdims=True)
    i2 = jnp.min(jnp.where(pm2 == p2, lane, big), axis=1, keepdims=True)
    denom = p1 + p2
    g1 = p_g * p1 / denom
    g2 = p_g * p2 / denom

    sel1 = lane == i1
    sel2 = lane == i2
    onehot = jnp.where(sel1 | sel2, 1.0, 0.0)
    r_i = lax.broadcasted_iota(jnp.int32, (tm, tm), 0)
    c_i = lax.broadcasted_iota(jnp.int32, (tm, tm), 1)
    tri = jnp.where(c_i < r_i, 1.0, 0.0).astype(BF16)
    before = carry_ref[...] + jnp.dot(tri, onehot.astype(BF16), preferred_element_type=F32)
    r1 = jnp.sum(jnp.where(sel1, before, 0.0), axis=1, keepdims=True)
    r2 = jnp.sum(jnp.where(sel2, before, 0.0), axis=1, keepdims=True)
    total = carry_ref[...] + jnp.sum(onehot, axis=0, keepdims=True)
    carry_ref[...] = total
    cnt_ref[...] = total

    info = jnp.where(lane == 0, i1 - EXPERT_LANE0,
                     jnp.where(lane == 1, i2 - EXPERT_LANE0,
                               jnp.where(lane == 2, r1, jnp.where(lane == 3, r2, 0.0))))
    info_ref[...] = info.astype(jnp.int32)
    gate_ref[...] = jnp.where(lane == 0, g1, jnp.where(lane == 1, g2, 0.0))


def _router(h, g, w_router):
    tp, d = h.shape
    tm = TOK_TILE
    row = lambda i: (i, 0)
    return pl.pallas_call(
        _router_kernel,
        grid=(tp // tm,),
        in_specs=[pl.BlockSpec((tm, d), row), _full((1, d)), _full((d, LANES))],
        out_specs=[
            pl.BlockSpec((tm, d), row),
            pl.BlockSpec((tm, LANES), row),
            pl.BlockSpec((tm, LANES), row),
            _full((1, LANES)),
        ],
        out_shape=[
            jax.ShapeDtypeStruct((tp, d), F32),
            jax.ShapeDtypeStruct((tp, LANES), jnp.int32),
            jax.ShapeDtypeStruct((tp, LANES), F32),
            jax.ShapeDtypeStruct((1, LANES), F32),
        ],
        scratch_shapes=[pltpu.VMEM((1, LANES), F32)],
        compiler_params=_cparams("arbitrary"),
        name="moe_router",
    )(h, g, w_router)


def _row_copy(src, s, dst, t, sem):
    return pltpu.make_async_copy(src.at[pl.ds(s, 1), :], dst.at[pl.ds(t, 1), :], sem)


def _dispatch_kernel(dest_ref, hn_ref, xs_in_ref, xs_ref, sem):
    del xs_in_ref
    n_tok = hn_ref.shape[0]

    def wait_one_token():
        _row_copy(hn_ref, 0, xs_ref, 0, sem.at[0]).wait()
        _row_copy(hn_ref, 0, xs_ref, 0, sem.at[1]).wait()

    def body(t, carry):
        _row_copy(hn_ref, t, xs_ref, dest_ref[2 * t], sem.at[0]).start()
        _row_copy(hn_ref, t, xs_ref, dest_ref[2 * t + 1], sem.at[1]).start()

        @pl.when(t >= DMA_WINDOW)
        def _():
            wait_one_token()

        return carry

    lax.fori_loop(0, n_tok, body, 0)

    def drain(t, carry):
        wait_one_token()
        return carry

    lax.fori_loop(0, min(DMA_WINDOW, n_tok), drain, 0)


def _dispatch(dest_flat, hn, n_slots):
    tp, d = hn.shape
    xs_init = jnp.zeros((n_slots, d), hn.dtype)
    any_spec = pl.BlockSpec(memory_space=pl.ANY)
    return pl.pallas_call(
        _dispatch_kernel,
        grid_spec=pltpu.PrefetchScalarGridSpec(
            num_scalar_prefetch=1,
            grid=(1,),
            in_specs=[any_spec, any_spec],
            out_specs=any_spec,
            scratch_shapes=[pltpu.SemaphoreType.DMA((2,))],
        ),
        out_shape=jax.ShapeDtypeStruct((n_slots, d), hn.dtype),
        input_output_aliases={2: 0},
        compiler_params=_cparams("arbitrary"),
        name="moe_dispatch",
    )(dest_flat, hn, xs_init)


def _ffn_kernel(blk_e_ref, nact_ref, xs_ref, wg_ref, wu_ref, wd_ref, ys_ref):
    del blk_e_ref

    @pl.when(pl.program_id(0) < nact_ref[0])
    def _():
        x = xs_ref[...].astype(BF16)
        gate = jnp.dot(x, wg_ref[...].astype(BF16), preferred_element_type=F32)
        up = jnp.dot(x, wu_ref[...].astype(BF16), preferred_element_type=F32)
        hh = (gate * jax.nn.sigmoid(gate) * up).astype(BF16)
        ys_ref[...] = jnp.dot(hh, wd_ref[...].astype(BF16), preferred_element_type=F32)


def _ffn(blk_expert, n_active, xs, w_gate, w_up, w_down):
    n_slots, d = xs.shape
    de = w_gate.shape[-1]
    bm = FFN_BLOCK

    def rows(b, be, na):
        return (jnp.minimum(b, na[0] - 1), 0)

    def w_idx(b, be, na):
        return (be[b], 0, 0)

    return pl.pallas_call(
        _ffn_kernel,
        grid_spec=pltpu.PrefetchScalarGridSpec(
            num_scalar_prefetch=2,
            grid=(n_slots // bm,),
            in_specs=[
                pl.BlockSpec((bm, d), rows),
                pl.BlockSpec((None, d, de), w_idx),
                pl.BlockSpec((None, d, de), w_idx),
                pl.BlockSpec((None, de, d), w_idx),
            ],
            out_specs=pl.BlockSpec((bm, d), rows),
        ),
        out_shape=jax.ShapeDtypeStruct((n_slots, d), F32),
        compiler_params=_cparams("arbitrary"),
        name="moe_ffn",
    )(blk_expert, n_active, xs, w_gate, w_up, w_down)


def _combine_kernel(dest_ref, h_ref, gate_ref, ys_ref, *rest, final_norm):
    if final_norm:
        fg_ref, o_ref, buf_a, buf_b, sem = rest
    else:
        o_ref, buf_a, buf_b, sem = rest
    tm = h_ref.shape[0]
    base = pl.program_id(0) * tm

    def issue(r, carry):
        _row_copy(ys_ref, dest_ref[2 * (base + r)], buf_a, r, sem.at[0]).start()
        _row_copy(ys_ref, dest_ref[2 * (base + r) + 1], buf_b, r, sem.at[1]).start()
        return carry

    lax.fori_loop(0, tm, issue, 0)

    def drain(r, carry):
        _row_copy(ys_ref, 0, buf_a, 0, sem.at[0]).wait()
        _row_copy(ys_ref, 0, buf_b, 0, sem.at[1]).wait()
        return carry

    lax.fori_loop(0, tm, drain, 0)

    gates = gate_ref[...]
    out = h_ref[...] + gates[:, 0:1] * buf_a[...] + gates[:, 1:2] * buf_b[...]
    if final_norm:
        out = _rms(out, fg_ref[...])
    o_ref[...] = out


def _combine(dest_flat, h, gates, ys, final_g):
    tp, d = h.shape
    tm = TOK_TILE
    row = lambda i, dest: (i, 0)
    in_specs = [
        pl.BlockSpec((tm, d), row),
        pl.BlockSpec((tm, LANES), row),
        pl.BlockSpec(memory_space=pl.ANY),
    ]
    args = [dest_flat, h, gates, ys]
    if final_g is not None:
        in_specs.append(pl.BlockSpec((1, d), lambda i, dest: (0, 0)))
        args.append(final_g)
    return pl.pallas_call(
        functools.partial(_combine_kernel, final_norm=final_g is not None),
        grid_spec=pltpu.PrefetchScalarGridSpec(
            num_scalar_prefetch=1,
            grid=(tp // tm,),
            in_specs=in_specs,
            out_specs=pl.BlockSpec((tm, d), row),
            scratch_shapes=[
                pltpu.VMEM((tm, d), F32),
                pltpu.VMEM((tm, d), F32),
                pltpu.SemaphoreType.DMA((2,)),
            ],
        ),
        out_shape=jax.ShapeDtypeStruct((tp, d), F32),
        compiler_params=_cparams("arbitrary"),
        name="moe_combine",
    )(*args)


def _hier_moe(h, norm_g, w_grp, w_exp, w_gate, w_up, w_down, final_g=None):
    tp, d = h.shape
    w_router = jnp.zeros((d, LANES), F32)
    w_router = w_router.at[:, :N_GROUPS].set(w_grp).at[:, EXPERT_LANE0:].set(w_exp)
    hn, info, gates, cnt = _router(h, norm_g, w_router)

    bm = FFN_BLOCK
    n_blocks = -(-(2 * tp + N_EXPERTS * (bm - 1)) // bm)
    counts = cnt[0, EXPERT_LANE0:].astype(jnp.int32)
    padded = ((counts + bm - 1) // bm) * bm
    pad_end = jnp.cumsum(padded)
    pad_start = pad_end - padded
    dest = pad_start[info[:, 0:2]] + info[:, 2:4]
    dest_flat = dest.reshape(-1).astype(jnp.int32)
    n_active = (pad_end[-1] // bm).astype(jnp.int32)
    blk = jnp.minimum(jnp.arange(n_blocks, dtype=jnp.int32), n_active - 1)
    blk_expert = jnp.sum((pad_end[None, :] <= (blk * bm)[:, None]).astype(jnp.int32), axis=1)
    blk_expert = jnp.minimum(blk_expert, N_EXPERTS - 1).astype(jnp.int32)

    xs = _dispatch(dest_flat, hn, n_blocks * bm)
    ys = _ffn(blk_expert, n_active.reshape(1), xs, w_gate, w_up, w_down)
    return _combine(dest_flat, h, gates, ys, final_g)


def _kv_proj_kernel(h_ref, g_ref, wd_ref, cg_ref, wu_ref, cos_ref, sin_ref, k_ref, v_ref):
    hn = _rms(h_ref[...], g_ref[...]).astype(BF16)
    a = jnp.dot(hn, wd_ref[...], preferred_element_type=F32)
    c_kv = _rms(a[:, :KV_RANK], cg_ref[...]).astype(BF16)
    k_rope = (a[:, KV_RANK:KV_RANK + LANES] * cos_ref[...]
              + a[:, KV_RANK + LANES:] * sin_ref[...]).astype(BF16)
    kv = jnp.dot(c_kv, wu_ref[...], preferred_element_type=F32)
    per_head = QK_NOPE_DIM + V_DIM
    for hd in range(N_HEADS):
        k_ref[hd, :, :QK_NOPE_DIM] = kv[:, hd * per_head:hd * per_head + QK_NOPE_DIM].astype(BF16)
        k_ref[hd, :, QK_NOPE_DIM:] = k_rope
        v_ref[hd] = kv[:, hd * per_head + QK_NOPE_DIM:(hd + 1) * per_head].astype(BF16)


def _kv_proj(h, g, w_dkv_ext, ckv_g, w_ukv, cos_t, sin_t):
    tp, d = h.shape
    tm = TOK_TILE
    row = lambda i: (i, 0)
    return pl.pallas_call(
        _kv_proj_kernel,
        grid=(tp // tm,),
        in_specs=[
            pl.BlockSpec((tm, d), row),
            _full((1, d)),
            _full(w_dkv_ext.shape),
            _full((1, KV_RANK)),
            _full(w_ukv.shape),
            pl.BlockSpec((tm, LANES), row),
            pl.BlockSpec((tm, LANES), row),
        ],
        out_specs=[
            pl.BlockSpec((N_HEADS, tm, HEAD_PAD), lambda i: (0, i, 0)),
            pl.BlockSpec((N_HEADS, tm, V_DIM), lambda i: (0, i, 0)),
        ],
        out_shape=[
            jax.ShapeDtypeStruct((N_HEADS, tp, HEAD_PAD), BF16),
            jax.ShapeDtypeStruct((N_HEADS, tp, V_DIM), BF16),
        ],
        compiler_params=_cparams("parallel"),
        name="kv_proj",
    )(h, g, w_dkv_ext, ckv_g, w_ukv, cos_t, sin_t)


def _q_proj_kernel(h_ref, g_ref, wd_ref, cg_ref, wu_ref, cos_ref, sin_ref, q_ref):
    hn = _rms(h_ref[...], g_ref[...]).astype(BF16)
    cq = _rms(jnp.dot(hn, wd_ref[...], preferred_element_type=F32), cg_ref[...]).astype(BF16)
    q = jnp.dot(cq, wu_ref[...], preferred_element_type=F32)
    cos_t = cos_ref[...]
    sin_t = sin_ref[...]
    swapped0 = N_HEADS * HEAD_PAD
    for hd in range(N_HEADS):
        lo = hd * HEAD_PAD
        q_ref[hd, :, :QK_NOPE_DIM] = q[:, lo:lo + QK_NOPE_DIM].astype(BF16)
        rope = (q[:, lo + QK_NOPE_DIM:lo + HEAD_PAD] * cos_t
                + q[:, swapped0 + hd * LANES:swapped0 + (hd + 1) * LANES] * sin_t)
        q_ref[hd, :, QK_NOPE_DIM:] = rope.astype(BF16)


def _q_proj(h, g, w_dq, cq_g, w_uq_ext, cos_t, sin_t):
    tp, d = h.shape
    tm = TOK_TILE
    row = lambda i: (i, 0)
    return pl.pallas_call(
        _q_proj_kernel,
        grid=(tp // tm,),
        in_specs=[
            pl.BlockSpec((tm, d), row),
            _full((1, d)),
            _full(w_dq.shape),
            _full((1, w_dq.shape[1])),
            _full(w_uq_ext.shape),
            pl.BlockSpec((tm, LANES), row),
            pl.BlockSpec((tm, LANES), row),
        ],
        out_specs=pl.BlockSpec((N_HEADS, tm, HEAD_PAD), lambda i: (0, i, 0)),
        out_shape=jax.ShapeDtypeStruct((N_HEADS, tp, HEAD_PAD), BF16),
        compiler_params=_cparams("parallel"),
        name="q_proj",
    )(h, g, w_dq, cq_g, w_uq_ext, cos_t, sin_t)


def _attn_kernel(q_ref, k_ref, v_ref, o_ref, m_ref, l_ref, acc_ref):
    bq = q_ref.shape[0]
    bk = k_ref.shape[0]
    qi = pl.program_id(1)
    kj = pl.program_id(2)
    q_lo = qi * bq
    k_lo = kj * bk

    @pl.when(kj == 0)
    def _():
        m_ref[...] = jnp.full_like(m_ref, -jnp.inf)
        l_ref[...] = jnp.zeros_like(l_ref)
        acc_ref[...] = jnp.zeros_like(acc_ref)

    def step(masked):
        s = lax.dot_general(q_ref[...], k_ref[...], (((1,), (1,)), ((), ())),
                            preferred_element_type=F32)
        if masked:
            qpos = q_lo + lax.broadcasted_iota(jnp.int32, s.shape, 0)
            kpos = k_lo + lax.broadcasted_iota(jnp.int32, s.shape, 1)
            s = jnp.where(kpos <= qpos, s, -jnp.inf)
        m_old = m_ref[...]
        m_new = jnp.maximum(m_old, jnp.max(s, axis=1, keepdims=True))
        alpha = jnp.exp(m_old - m_new)
        p = jnp.exp(s - m_new)
        l_ref[...] = alpha * l_ref[...] + jnp.sum(p, axis=1, keepdims=True)
        acc_ref[...] = alpha * acc_ref[...] + jnp.dot(p.astype(BF16), v_ref[...],
                                                      preferred_element_type=F32)
        m_ref[...] = m_new

    fully_visible = k_lo + bk - 1 <= q_lo
    partly_visible = (k_lo <= q_lo + bq - 1) & jnp.logical_not(fully_visible)

    @pl.when(fully_visible)
    def _():
        step(False)

    @pl.when(partly_visible)
    def _():
        step(True)

    @pl.when(kj == pl.num_programs(2) - 1)
    def _():
        o_ref[...] = (acc_ref[...] / l_ref[...]).astype(o_ref.dtype)


def _attention(q, k, v, bq, bk):
    n_heads, tp, dq = q.shape
    dv = v.shape[-1]

    def kv_idx(hd, qi, kj):
        return (hd, jnp.minimum(kj, (qi * bq + bq - 1) // bk), 0)

    return pl.pallas_call(
        _attn_kernel,
        grid=(n_heads, tp // bq, tp // bk),
        in_specs=[
            pl.BlockSpec((None, bq, dq), lambda hd, qi, kj: (hd, qi, 0)),
            pl.BlockSpec((None, bk, dq), kv_idx),
            pl.BlockSpec((None, bk, dv), kv_idx),
        ],
        out_specs=pl.BlockSpec((bq, dv), lambda hd, qi, kj: (qi, hd)),
        out_shape=jax.ShapeDtypeStruct((tp, n_heads * dv), BF16),
        scratch_shapes=[
            pltpu.VMEM((bq, 1), F32),
            pltpu.VMEM((bq, 1), F32),
            pltpu.VMEM((bq, dv), F32),
        ],
        compiler_params=_cparams("parallel", "parallel", "arbitrary"),
        name="mla_attention",
    )(q, k, v)


def _out_proj_kernel(o_ref, w_ref, h_ref, out_ref):
    out_ref[...] = h_ref[...] + jnp.dot(o_ref[...], w_ref[...], preferred_element_type=F32)


def _out_proj(o, w_o, h):
    tp, d = h.shape
    tm = TOK_TILE
    row = lambda i: (i, 0)
    return pl.pallas_call(
        _out_proj_kernel,
        grid=(tp // tm,),
        in_specs=[pl.BlockSpec((tm, o.shape[1]), row), _full(w_o.shape), pl.BlockSpec((tm, d), row)],
        out_specs=pl.BlockSpec((tm, d), row),
        out_shape=jax.ShapeDtypeStruct((tp, d), F32),
        compiler_params=_cparams("parallel"),
        name="attn_out_proj",
    )(o, w_o, h)


def _rope_tables(tp):
    inv = ROPE_THETA ** (-jnp.arange(0, QK_ROPE_DIM, 2, dtype=F32) / QK_ROPE_DIM)
    ang = jnp.arange(tp, dtype=F32)[:, None] * inv[None, :]
    cos, sin = jnp.cos(ang), jnp.sin(ang)
    zeros = jnp.zeros((tp, LANES - QK_ROPE_DIM), F32)
    return (jnp.concatenate([cos, cos, zeros], axis=1),
            jnp.concatenate([-sin, sin, zeros], axis=1))


def _swap_halves(w):
    half = w.shape[-1] // 2
    return jnp.concatenate([w[..., half:], w[..., :half]], axis=-1)


def _attn_block(tp):
    for b in (640, 512, 256):
        if tp % b == 0:
            return b
    raise ValueError(f"no attention block size divides {tp}")


def kernel(x, meta_tokens, norm_mix_g, norm_ffn_g, conv_w_in, conv_w, conv_w_out, kv_norm_g, w_dkv,
           ckv_norm_g, w_ukv, w_dq, cq_norm_g, w_uq, w_o, w_grp, w_exp, w_gate, w_up, w_down,
           final_norm_g):
    bsz, seq, d = x.shape
    assert bsz == 1, "the token-major kernels assume a single sequence"
    assert norm_mix_g.shape[0] == 2, "one short-conv layer followed by one MLA layer"
    t = N_META + seq
    tp = -(-t // TOK_TILE) * TOK_TILE
    h = jnp.concatenate([meta_tokens.astype(x.dtype), x[0], jnp.zeros((tp - t, d), x.dtype)], axis=0)

    h = _conv_mixer(h, norm_mix_g[0][None], conv_w_in[0].astype(BF16), conv_w[0, :, 0, :],
                    conv_w_out[0].astype(BF16))
    h = _hier_moe(h, norm_ffn_g[0][None], w_grp[0], w_exp[0], w_gate[0], w_up[0], w_down[0])

    cos_t, sin_t = _rope_tables(tp)
    zeros_r = jnp.zeros((d, LANES - QK_ROPE_DIM), F32)
    w_kr = w_dkv[:, KV_RANK:]
    w_dkv_ext = jnp.concatenate([w_dkv[:, :KV_RANK], w_kr, zeros_r, _swap_halves(w_kr), zeros_r],
                                axis=1).astype(BF16)
    k, v = _kv_proj(h, kv_norm_g[None], w_dkv_ext, ckv_norm_g[None], w_ukv.astype(BF16), cos_t, sin_t)

    q_rank = w_dq.shape[-1]
    scale = (QK_NOPE_DIM + QK_ROPE_DIM) ** -0.5
    wq = w_uq[0].reshape(q_rank, N_HEADS, QK_NOPE_DIM + QK_ROPE_DIM) * scale
    w_qr = wq[:, :, QK_NOPE_DIM:]
    zeros_q = jnp.zeros((q_rank, N_HEADS, LANES - QK_ROPE_DIM), F32)
    w_uq_ext = jnp.concatenate([
        jnp.concatenate([wq[:, :, :QK_NOPE_DIM], w_qr, zeros_q], axis=-1).reshape(q_rank, -1),
        jnp.concatenate([_swap_halves(w_qr), zeros_q], axis=-1).reshape(q_rank, -1),
    ], axis=1).astype(BF16)
    q = _q_proj(h, norm_mix_g[1][None], w_dq[0].astype(BF16), cq_norm_g[0][None], w_uq_ext, cos_t, sin_t)
    blk = _attn_block(tp)
    o = _attention(q, k, v, blk, blk)
    h = _out_proj(o, w_o[0].astype(BF16), h)
    h = _hier_moe(h, norm_ffn_g[1][None], w_grp[1], w_exp[1], w_gate[1], w_up[1], w_down[1],
                  final_g=final_norm_g[None])
    return h[N_META:t][None]
```

```python
import functools

import jax
import jax.numpy as jnp
from jax import lax
from jax.experimental import pallas as pl
from jax.experimental.pallas import tpu as pltpu

F32 = jnp.float32
BF16 = jnp.bfloat16

N_META = 16
N_HEADS = 8
QK_NOPE_DIM = 128
QK_ROPE_DIM = 64
V_DIM = 128
KV_RANK = 256
ROPE_THETA = 10000.0
N_GROUPS = 8
EXPERTS_PER_GROUP = 8
N_EXPERTS = N_GROUPS * EXPERTS_PER_GROUP
NORM_EPS = 1e-6
LOG2_E = 1.4426950408889634

LANES = 128
HEAD_PAD = 256
TOK_TILE = 256
FFN_BLOCK = 256
EXPERT_LANE0 = 64
HEADS_PER_STEP = 2
ATTN_BLOCK = 1280
VMEM_LIMIT = 48 * 1024 * 1024


def _rms(x, g):
    return x * lax.rsqrt(jnp.mean(x * x, axis=-1, keepdims=True) + NORM_EPS) * g


def _cparams(*sem):
    return pltpu.CompilerParams(dimension_semantics=sem, vmem_limit_bytes=VMEM_LIMIT)


def _full(shape):
    return pl.BlockSpec(shape, lambda *_: (0,) * len(shape))


def _conv_mixer_kernel(x_ref, g_ref, win_ref, cw_ref, wout_ref, o_ref, carry_ref):
    d = x_ref.shape[1]
    tm = x_ref.shape[0]

    @pl.when(pl.program_id(0) == 0)
    def _():
        carry_ref[...] = jnp.zeros_like(carry_ref)

    x = x_ref[...]
    hn = _rms(x, g_ref[...]).astype(BF16)
    bcu = jnp.dot(hn, win_ref[...], preferred_element_type=F32)
    b, c, u = bcu[:, :d], bcu[:, d:2 * d], bcu[:, 2 * d:]
    z = c * u
    row = lax.broadcasted_iota(jnp.int32, z.shape, 0)
    prev1 = carry_ref[7:8, :]
    prev2 = carry_ref[6:7, :]
    z1 = jnp.where(row == 0, prev1, pltpu.roll(z, 1, axis=0))
    z2 = jnp.where(row == 0, prev2, jnp.where(row == 1, prev1, pltpu.roll(z, 2, axis=0)))
    conv = cw_ref[0:1, :] * z2 + cw_ref[1:2, :] * z1 + cw_ref[2:3, :] * z
    carry_ref[...] = z[tm - 8:, :]
    y = (b * conv).astype(BF16)
    o_ref[...] = x + jnp.dot(y, wout_ref[...], preferred_element_type=F32)


def _conv_mixer(h, g, w_in, conv_w, w_out):
    tp, d = h.shape
    tm = TOK_TILE
    return pl.pallas_call(
        _conv_mixer_kernel,
        grid=(tp // tm,),
        in_specs=[
            pl.BlockSpec((tm, d), lambda i: (i, 0)),
            _full((1, d)),
            _full((d, 3 * d)),
            _full((3, d)),
            _full((d, d)),
        ],
        out_specs=pl.BlockSpec((tm, d), lambda i: (i, 0)),
        out_shape=jax.ShapeDtypeStruct((tp, d), F32),
        scratch_shapes=[pltpu.VMEM((8, d), F32)],
        compiler_params=_cparams("arbitrary"),
        name="conv_mixer",
    )(h, g, w_in, conv_w, w_out)


def _router_kernel(h_ref, g_ref, wr_ref, hn_ref, info_ref, gate_ref, cnt_ref, carry_ref):
    tm = h_ref.shape[0]

    @pl.when(pl.program_id(0) == 0)
    def _():
        carry_ref[...] = jnp.zeros_like(carry_ref)

    hn = _rms(h_ref[...], g_ref[...])
    hn_ref[...] = hn
    logits = jnp.dot(hn, wr_ref[...], precision=lax.Precision.HIGHEST,
                     preferred_element_type=F32)
    lane = lax.broadcasted_iota(jnp.int32, logits.shape, 1).astype(F32)
    neg = jnp.float32(-jnp.inf)
    big = jnp.float32(2 * LANES)

    is_g = lane < N_GROUPS
    gmax = jnp.max(jnp.where(is_g, logits, neg), axis=1, keepdims=True)
    grp = jnp.min(jnp.where(is_g & (logits == gmax), lane, big), axis=1, keepdims=True)
    gsum = jnp.sum(jnp.where(is_g, jnp.exp(logits - gmax), 0.0), axis=1, keepdims=True)
    p_g = 1.0 / gsum

    lo = EXPERT_LANE0 + EXPERTS_PER_GROUP * grp
    in_grp = (lane >= lo) & (lane < lo + EXPERTS_PER_GROUP)
    emax = jnp.max(jnp.where(in_grp, logits, neg), axis=1, keepdims=True)
    ex = jnp.where(in_grp, jnp.exp(logits - emax), 0.0)
    p = ex / jnp.sum(ex, axis=1, keepdims=True)
    pm1 = jnp.where(in_grp, p, -1.0)
    p1 = jnp.max(pm1, axis=1, keepdims=True)
    i1 = jnp.min(jnp.where(pm1 == p1, lane, big), axis=1, keepdims=True)
    pm2 = jnp.where(lane == i1, -1.0, pm1)
    p2 = jnp.max(pm2, axis=1, keepdims=True)
    i2 = jnp.min(jnp.where(pm2 == p2, lane, big), axis=1, keepdims=True)
    denom = p1 + p2
    g1 = p_g * p1 / denom
    g2 = p_g * p2 / denom

    sel1 = lane == i1
    sel2 = lane == i2
    onehot = jnp.where(sel1 | sel2, 1.0, 0.0)
    r_i = lax.broadcasted_iota(jnp.int32, (tm, tm), 0)
    c_i = lax.broadcasted_iota(jnp.int32, (tm, tm), 1)
    tri = jnp.where(c_i < r_i, 1.0, 0.0).astype(BF16)
    before = carry_ref[...] + jnp.dot(tri, onehot.astype(BF16), preferred_element_type=F32)
    r1 = jnp.sum(jnp.where(sel1, before, 0.0), axis=1, keepdims=True)
    r2 = jnp.sum(jnp.where(sel2, before, 0.0), axis=1, keepdims=True)
    total = carry_ref[...] + jnp.sum(onehot, axis=0, keepdims=True)
    carry_ref[...] = total
    cnt_ref[...] = total

    info = jnp.where(lane == 0, i1 - EXPERT_LANE0,
                     jnp.where(lane == 1, i2 - EXPERT_LANE0,
                               jnp.where(lane == 2, r1, jnp.where(lane == 3, r2, 0.0))))
    info_ref[...] = info.astype(jnp.int32)
    gate_ref[...] = jnp.where(lane == 0, g1, jnp.where(lane == 1, g2, 0.0))


def _router(h, g, w_router):
    tp, d = h.shape
    tm = TOK_TILE
    row = lambda i: (i, 0)
    return pl.pallas_call(
        _router_kernel,
        grid=(tp // tm,),
        in_specs=[pl.BlockSpec((tm, d), row), _full((1, d)), _full((d, LANES))],
        out_specs=[
            pl.BlockSpec((tm, d), row),
            pl.BlockSpec((tm, LANES), row),
            pl.BlockSpec((tm, LANES), row),
            _full((1, LANES)),
        ],
        out_shape=[
            jax.ShapeDtypeStruct((tp, d), F32),
            jax.ShapeDtypeStruct((tp, LANES), jnp.int32),
            jax.ShapeDtypeStruct((tp, LANES), F32),
            jax.ShapeDtypeStruct((1, LANES), F32),
        ],
        scratch_shapes=[pltpu.VMEM((1, LANES), F32)],
        compiler_params=_cparams("arbitrary"),
        name="moe_router",
    )(h, g, w_router)


def _row_copy(src, s, dst, t, sem):
    return pltpu.make_async_copy(src.at[pl.ds(s, 1), :], dst.at[pl.ds(t, 1), :], sem)


def _dispatch_kernel(dest_ref, hn_ref, xs_in_ref, xs_ref, sem):
    del xs_in_ref
    tm = hn_ref.shape[0]
    base = pl.program_id(0) * tm

    def issue(r, carry):
        _row_copy(hn_ref, r, xs_ref, dest_ref[2 * (base + r)], sem.at[0]).start()
        _row_copy(hn_ref, r, xs_ref, dest_ref[2 * (base + r) + 1], sem.at[1]).start()
        return carry

    lax.fori_loop(0, tm, issue, 0, unroll=8)

    def drain(r, carry):
        _row_copy(hn_ref, 0, xs_ref, 0, sem.at[0]).wait()
        _row_copy(hn_ref, 0, xs_ref, 0, sem.at[1]).wait()
        return carry

    lax.fori_loop(0, tm, drain, 0, unroll=8)


def _dispatch(dest_flat, hn, n_slots):
    tp, d = hn.shape
    tm = TOK_TILE
    xs_init = jnp.zeros((n_slots, d), hn.dtype)
    any_spec = pl.BlockSpec(memory_space=pl.ANY)
    return pl.pallas_call(
        _dispatch_kernel,
        grid_spec=pltpu.PrefetchScalarGridSpec(
            num_scalar_prefetch=1,
            grid=(tp // tm,),
            in_specs=[pl.BlockSpec((tm, d), lambda i, dest: (i, 0)), any_spec],
            out_specs=any_spec,
            scratch_shapes=[pltpu.SemaphoreType.DMA((2,))],
        ),
        out_shape=jax.ShapeDtypeStruct((n_slots, d), hn.dtype),
        input_output_aliases={2: 0},
        compiler_params=_cparams("arbitrary"),
        name="moe_dispatch",
    )(dest_flat, hn, xs_init)


def _ffn_kernel(blk_e_ref, nact_ref, xs_ref, wg_ref, wu_ref, wd_ref, ys_ref):
    del blk_e_ref

    @pl.when(pl.program_id(0) < nact_ref[0])
    def _():
        x = xs_ref[...].astype(BF16)
        gate = jnp.dot(x, wg_ref[...].astype(BF16), preferred_element_type=F32)
        up = jnp.dot(x, wu_ref[...].astype(BF16), preferred_element_type=F32)
        hh = (gate * jax.nn.sigmoid(gate) * up).astype(BF16)
        ys_ref[...] = jnp.dot(hh, wd_ref[...].astype(BF16), preferred_element_type=F32)

    @pl.when(pl.program_id(0) >= nact_ref[0])
    def _():
        ys_ref[...] = jnp.zeros_like(ys_ref)


def _ffn(blk_expert, n_active, xs, w_gate, w_up, w_down):
    n_slots, d = xs.shape
    de = w_gate.shape[-1]
    bm = FFN_BLOCK

    def rows(b, be, na):
        return (jnp.minimum(b, na[0] - 1), 0)

    def w_idx(b, be, na):
        return (be[b], 0, 0)

    return pl.pallas_call(
        _ffn_kernel,
        grid_spec=pltpu.PrefetchScalarGridSpec(
            num_scalar_prefetch=2,
            grid=(n_slots // bm,),
            in_specs=[
                pl.BlockSpec((bm, d), rows),
                pl.BlockSpec((None, d, de), w_idx),
                pl.BlockSpec((None, d, de), w_idx),
                pl.BlockSpec((None, de, d), w_idx),
            ],
            out_specs=pl.BlockSpec((bm, d), lambda b, be, na: (b, 0)),
        ),
        out_shape=jax.ShapeDtypeStruct((n_slots, d), F32),
        compiler_params=_cparams("arbitrary"),
        name="moe_ffn",
    )(blk_expert, n_active, xs, w_gate, w_up, w_down)


def _combine_kernel(dest_ref, h_ref, gate_ref, ys_ref, *rest, final_norm):
    if final_norm:
        fg_ref, o_ref, buf_a, buf_b, sem = rest
    else:
        o_ref, buf_a, buf_b, sem = rest
    tm = h_ref.shape[0]
    base = pl.program_id(0) * tm

    def issue(r, carry):
        _row_copy(ys_ref, dest_ref[2 * (base + r)], buf_a, r, sem.at[0]).start()
        _row_copy(ys_ref, dest_ref[2 * (base + r) + 1], buf_b, r, sem.at[1]).start()
        return carry

    lax.fori_loop(0, tm, issue, 0, unroll=8)

    def drain(r, carry):
        _row_copy(ys_ref, 0, buf_a, 0, sem.at[0]).wait()
        _row_copy(ys_ref, 0, buf_b, 0, sem.at[1]).wait()
        return carry

    lax.fori_loop(0, tm, drain, 0, unroll=8)

    gates = gate_ref[...]
    out = h_ref[...] + gates[:, 0:1] * buf_a[...] + gates[:, 1:2] * buf_b[...]
    if final_norm:
        out = _rms(out, fg_ref[...])
    o_ref[...] = out


def _combine(dest_flat, h, gates, ys, final_g):
    tp, d = h.shape
    tm = TOK_TILE
    row = lambda i, dest: (i, 0)
    in_specs = [
        pl.BlockSpec((tm, d), row),
        pl.BlockSpec((tm, LANES), row),
        pl.BlockSpec(memory_space=pl.ANY),
    ]
    args = [dest_flat, h, gates, ys]
    if final_g is not None:
        in_specs.append(pl.BlockSpec((1, d), lambda i, dest: (0, 0)))
        args.append(final_g)
    return pl.pallas_call(
        functools.partial(_combine_kernel, final_norm=final_g is not None),
        grid_spec=pltpu.PrefetchScalarGridSpec(
            num_scalar_prefetch=1,
            grid=(tp // tm,),
            in_specs=in_specs,
            out_specs=pl.BlockSpec((tm, d), row),
            scratch_shapes=[
                pltpu.VMEM((tm, d), F32),
                pltpu.VMEM((tm, d), F32),
                pltpu.SemaphoreType.DMA((2,)),
            ],
        ),
        out_shape=jax.ShapeDtypeStruct((tp, d), F32),
        compiler_params=_cparams("arbitrary"),
        name="moe_combine",
    )(*args)


def _hier_moe(h, norm_g, w_grp, w_exp, w_gate, w_up, w_down, final_g=None):
    tp, d = h.shape
    w_router = jnp.zeros((d, LANES), F32)
    w_router = w_router.at[:, :N_GROUPS].set(w_grp).at[:, EXPERT_LANE0:].set(w_exp)
    hn, info, gates, cnt = _router(h, norm_g, w_router)

    bm = FFN_BLOCK
    n_blocks = -(-(2 * tp + N_EXPERTS * (bm - 1)) // bm)
    counts = cnt[0, EXPERT_LANE0:].astype(jnp.int32)
    padded = ((counts + bm - 1) // bm) * bm
    pad_end = jnp.cumsum(padded)
    pad_start = pad_end - padded
    dest = pad_start[info[:, 0:2]] + info[:, 2:4]
    dest_flat = dest.reshape(-1).astype(jnp.int32)
    n_active = (pad_end[-1] // bm).astype(jnp.int32)
    blk = jnp.minimum(jnp.arange(n_blocks, dtype=jnp.int32), n_active - 1)
    blk_expert = jnp.sum((pad_end[None, :] <= (blk * bm)[:, None]).astype(jnp.int32), axis=1)
    blk_expert = jnp.minimum(blk_expert, N_EXPERTS - 1).astype(jnp.int32)

    xs = _dispatch(dest_flat, hn, n_blocks * bm)
    ys = _ffn(blk_expert, n_active.reshape(1), xs, w_gate, w_up, w_down)
    return _combine(dest_flat, h, gates, ys, final_g)


def _kv_proj_kernel(h_ref, g_ref, wd_ref, cg_ref, wu_ref, cos_ref, sin_ref, k_ref, vt_ref):
    hn = _rms(h_ref[...], g_ref[...]).astype(BF16)
    a = jnp.dot(hn, wd_ref[...], preferred_element_type=F32)
    c_kv = _rms(a[:, :KV_RANK], cg_ref[...]).astype(BF16)
    k_rope = (a[:, KV_RANK:KV_RANK + LANES] * cos_ref[...]
              + a[:, KV_RANK + LANES:] * sin_ref[...]).astype(BF16)
    kv = jnp.dot(c_kv, wu_ref[...], preferred_element_type=F32)
    per_head = QK_NOPE_DIM + V_DIM
    for hd in range(N_HEADS):
        k_ref[hd, :, :QK_NOPE_DIM] = kv[:, hd * per_head:hd * per_head + QK_NOPE_DIM].astype(BF16)
        k_ref[hd, :, QK_NOPE_DIM:] = k_rope
        vt_ref[hd] = kv[:, hd * per_head + QK_NOPE_DIM:(hd + 1) * per_head].T.astype(BF16)


def _kv_proj(h, g, w_dkv_ext, ckv_g, w_ukv, cos_t, sin_t):
    tp, d = h.shape
    tm = TOK_TILE
    row = lambda i: (i, 0)
    return pl.pallas_call(
        _kv_proj_kernel,
        grid=(tp // tm,),
        in_specs=[
            pl.BlockSpec((tm, d), row),
            _full((1, d)),
            _full(w_dkv_ext.shape),
            _full((1, KV_RANK)),
            _full(w_ukv.shape),
            pl.BlockSpec((tm, LANES), row),
            pl.BlockSpec((tm, LANES), row),
        ],
        out_specs=[
            pl.BlockSpec((N_HEADS, tm, HEAD_PAD), lambda i: (0, i, 0)),
            pl.BlockSpec((N_HEADS, V_DIM, tm), lambda i: (0, 0, i)),
        ],
        out_shape=[
            jax.ShapeDtypeStruct((N_HEADS, tp, HEAD_PAD), BF16),
            jax.ShapeDtypeStruct((N_HEADS, V_DIM, tp), BF16),
        ],
        compiler_params=_cparams("parallel"),
        name="kv_proj",
    )(h, g, w_dkv_ext, ckv_g, w_ukv, cos_t, sin_t)


def _q_proj_kernel(h_ref, g_ref, wd_ref, cg_ref, wu_ref, cos_ref, sin_ref, q_ref):
    hn = _rms(h_ref[...], g_ref[...]).astype(BF16)
    cq = _rms(jnp.dot(hn, wd_ref[...], preferred_element_type=F32), cg_ref[...]).astype(BF16)
    q = jnp.dot(cq, wu_ref[...], preferred_element_type=F32)
    cos_t = cos_ref[...]
    sin_t = sin_ref[...]
    swapped0 = N_HEADS * HEAD_PAD
    for hd in range(N_HEADS):
        lo = hd * HEAD_PAD
        q_ref[hd, :, :QK_NOPE_DIM] = q[:, lo:lo + QK_NOPE_DIM].astype(BF16)
        rope = (q[:, lo + QK_NOPE_DIM:lo + HEAD_PAD] * cos_t
                + q[:, swapped0 + hd * LANES:swapped0 + (hd + 1) * LANES] * sin_t)
        q_ref[hd, :, QK_NOPE_DIM:] = rope.astype(BF16)


def _q_proj(h, g, w_dq, cq_g, w_uq_ext, cos_t, sin_t):
    tp, d = h.shape
    tm = TOK_TILE
    row = lambda i: (i, 0)
    return pl.pallas_call(
        _q_proj_kernel,
        grid=(tp // tm,),
        in_specs=[
            pl.BlockSpec((tm, d), row),
            _full((1, d)),
            _full(w_dq.shape),
            _full((1, w_dq.shape[1])),
            _full(w_uq_ext.shape),
            pl.BlockSpec((tm, LANES), row),
            pl.BlockSpec((tm, LANES), row),
        ],
        out_specs=pl.BlockSpec((N_HEADS, tm, HEAD_PAD), lambda i: (0, i, 0)),
        out_shape=jax.ShapeDtypeStruct((N_HEADS, tp, HEAD_PAD), BF16),
        compiler_params=_cparams("parallel"),
        name="q_proj",
    )(h, g, w_dq, cq_g, w_uq_ext, cos_t, sin_t)


def _attn_kernel(qi_ref, kj_ref, q_ref, k_ref, vt_ref, o_ref, m_ref, l_ref, acc_ref):
    n_h, bq, _ = q_ref.shape
    bk = k_ref.shape[1]
    dv = vt_ref.shape[1]
    step_id = pl.program_id(1)
    qi = qi_ref[step_id]
    kj = kj_ref[step_id]
    q_lo = qi * bq
    k_lo = kj * bk

    @pl.when(kj == 0)
    def _():
        m_ref[...] = jnp.full_like(m_ref, -jnp.inf)
        l_ref[...] = jnp.zeros_like(l_ref)
        acc_ref[...] = jnp.zeros_like(acc_ref)

    def step(masked):
        for hd in range(n_h):
            st = lax.dot_general(k_ref[hd], q_ref[hd], (((1,), (1,)), ((), ())),
                                 preferred_element_type=F32)
            if masked:
                kpos = k_lo + lax.broadcasted_iota(jnp.int32, st.shape, 0)
                qpos = q_lo + lax.broadcasted_iota(jnp.int32, st.shape, 1)
                st = jnp.where(kpos <= qpos, st, -jnp.inf)
            m_old = m_ref[hd]
            m_new = jnp.maximum(m_old, jnp.max(st, axis=0, keepdims=True))
            alpha = jnp.exp2(m_old - m_new)
            pt = jnp.exp2(st - m_new)
            l_ref[hd] = alpha * l_ref[hd] + jnp.sum(pt, axis=0, keepdims=True)
            acc_ref[hd] = alpha * acc_ref[hd] + jnp.dot(vt_ref[hd], pt.astype(BF16),
                                                        preferred_element_type=F32)
            m_ref[hd] = m_new

    fully_visible = k_lo + bk - 1 <= q_lo

    @pl.when(fully_visible)
    def _():
        step(False)

    @pl.when(jnp.logical_not(fully_visible))
    def _():
        step(True)

    @pl.when(kj == (q_lo + bq - 1) // bk)
    def _():
        for hd in range(n_h):
            o_ref[:, hd * dv:(hd + 1) * dv] = (acc_ref[hd] / l_ref[hd]).T.astype(o_ref.dtype)


def _attention(q, k, vt, bq, bk):
    n_heads, tp, dq = q.shape
    dv = vt.shape[1]
    hps = HEADS_PER_STEP
    pairs = [(qi, kj) for qi in range(tp // bq) for kj in range((qi * bq + bq - 1) // bk + 1)]
    qi_tab = jnp.array([pq for pq, _ in pairs], jnp.int32)
    kj_tab = jnp.array([pk for _, pk in pairs], jnp.int32)

    return pl.pallas_call(
        _attn_kernel,
        grid_spec=pltpu.PrefetchScalarGridSpec(
            num_scalar_prefetch=2,
            grid=(n_heads // hps, len(pairs)),
            in_specs=[
                pl.BlockSpec((hps, bq, dq), lambda hp, st, qt, kt: (hp, qt[st], 0)),
                pl.BlockSpec((hps, bk, dq), lambda hp, st, qt, kt: (hp, kt[st], 0)),
                pl.BlockSpec((hps, dv, bk), lambda hp, st, qt, kt: (hp, 0, kt[st])),
            ],
            out_specs=pl.BlockSpec((bq, hps * dv), lambda hp, st, qt, kt: (qt[st], hp)),
            scratch_shapes=[
                pltpu.VMEM((hps, 1, bq), F32),
                pltpu.VMEM((hps, 1, bq), F32),
                pltpu.VMEM((hps, dv, bq), F32),
            ],
        ),
        out_shape=jax.ShapeDtypeStruct((tp, n_heads * dv), BF16),
        compiler_params=_cparams("parallel", "arbitrary"),
        name="mla_attention",
    )(qi_tab, kj_tab, q, k, vt)


def _out_proj_kernel(o_ref, w_ref, h_ref, out_ref):
    out_ref[...] = h_ref[...] + jnp.dot(o_ref[...], w_ref[...], preferred_element_type=F32)


def _out_proj(o, w_o, h):
    tp, d = h.shape
    tm = TOK_TILE
    row = lambda i: (i, 0)
    return pl.pallas_call(
        _out_proj_kernel,
        grid=(tp // tm,),
        in_specs=[pl.BlockSpec((tm, o.shape[1]), row), _full(w_o.shape), pl.BlockSpec((tm, d), row)],
        out_specs=pl.BlockSpec((tm, d), row),
        out_shape=jax.ShapeDtypeStruct((tp, d), F32),
        compiler_params=_cparams("parallel"),
        name="attn_out_proj",
    )(o, w_o, h)


def _rope_tables(tp):
    inv = ROPE_THETA ** (-jnp.arange(0, QK_ROPE_DIM, 2, dtype=F32) / QK_ROPE_DIM)
    ang = jnp.arange(tp, dtype=F32)[:, None] * inv[None, :]
    cos, sin = jnp.cos(ang), jnp.sin(ang)
    zeros = jnp.zeros((tp, LANES - QK_ROPE_DIM), F32)
    return (jnp.concatenate([cos, cos, zeros], axis=1),
            jnp.concatenate([-sin, sin, zeros], axis=1))


def _swap_halves(w):
    half = w.shape[-1] // 2
    return jnp.concatenate([w[..., half:], w[..., :half]], axis=-1)


def _attn_blocks(tp):
    blk = next(b for b in (ATTN_BLOCK, 512, 256) if tp % b == 0)
    return blk, blk


def kernel(x, meta_tokens, norm_mix_g, norm_ffn_g, conv_w_in, conv_w, conv_w_out, kv_norm_g, w_dkv,
           ckv_norm_g, w_ukv, w_dq, cq_norm_g, w_uq, w_o, w_grp, w_exp, w_gate, w_up, w_down,
           final_norm_g):
    bsz, seq, d = x.shape
    assert bsz == 1, "the token-major kernels assume a single sequence"
    assert norm_mix_g.shape[0] == 2, "one short-conv layer followed by one MLA layer"
    t = N_META + seq
    tp = -(-t // TOK_TILE) * TOK_TILE
    h = jnp.concatenate([meta_tokens.astype(x.dtype), x[0], jnp.zeros((tp - t, d), x.dtype)], axis=0)

    h = _conv_mixer(h, norm_mix_g[0][None], conv_w_in[0].astype(BF16), conv_w[0, :, 0, :],
                    conv_w_out[0].astype(BF16))
    h = _hier_moe(h, norm_ffn_g[0][None], w_grp[0], w_exp[0], w_gate[0], w_up[0], w_down[0])

    cos_t, sin_t = _rope_tables(tp)
    zeros_r = jnp.zeros((d, LANES - QK_ROPE_DIM), F32)
    w_kr = w_dkv[:, KV_RANK:]
    w_dkv_ext = jnp.concatenate([w_dkv[:, :KV_RANK], w_kr, zeros_r, _swap_halves(w_kr), zeros_r],
                                axis=1).astype(BF16)
    k, vt = _kv_proj(h, kv_norm_g[None], w_dkv_ext, ckv_norm_g[None], w_ukv.astype(BF16), cos_t, sin_t)

    q_rank = w_dq.shape[-1]
    scale = (QK_NOPE_DIM + QK_ROPE_DIM) ** -0.5 * LOG2_E
    wq = w_uq[0].reshape(q_rank, N_HEADS, QK_NOPE_DIM + QK_ROPE_DIM) * scale
    w_qr = wq[:, :, QK_NOPE_DIM:]
    zeros_q = jnp.zeros((q_rank, N_HEADS, LANES - QK_ROPE_DIM), F32)
    w_uq_ext = jnp.concatenate([
        jnp.concatenate([wq[:, :, :QK_NOPE_DIM], w_qr, zeros_q], axis=-1).reshape(q_rank, -1),
        jnp.concatenate([_swap_halves(w_qr), zeros_q], axis=-1).reshape(q_rank, -1),
    ], axis=1).astype(BF16)
    q = _q_proj(h, norm_mix_g[1][None], w_dq[0].astype(BF16), cq_norm_g[0][None], w_uq_ext, cos_t, sin_t)
    o = _attention(q, k, vt, *_attn_blocks(tp))
    h = _out_proj(o, w_o[0].astype(BF16), h)
    h = _hier_moe(h, norm_ffn_g[1][None], w_grp[1], w_exp[1], w_gate[1], w_up[1], w_down[1],
                  final_g=final_norm_g[None])
    return h[N_META:t][None]
```

```python
import functools

import jax
import jax.numpy as jnp
from jax import lax
from jax.experimental import pallas as pl
from jax.experimental.pallas import tpu as pltpu

F32 = jnp.float32
BF16 = jnp.bfloat16

N_META = 16
N_HEADS = 8
QK_NOPE_DIM = 128
QK_ROPE_DIM = 64
V_DIM = 128
KV_RANK = 256
ROPE_THETA = 10000.0
N_GROUPS = 8
EXPERTS_PER_GROUP = 8
N_EXPERTS = N_GROUPS * EXPERTS_PER_GROUP
NORM_EPS = 1e-6
LOG2_E = 1.4426950408889634

LANES = 128
HEAD_PAD = 256
TOK_TILE = 256
FFN_BLOCK = 256
EXPERT_ROW0 = 64
HEADS_PER_STEP = 2
ATTN_BLOCK = 1280
VMEM_LIMIT = 48 * 1024 * 1024


def _rms(x, g):
    return x * lax.rsqrt(jnp.mean(x * x, axis=-1, keepdims=True) + NORM_EPS) * g


def _cparams(*sem):
    return pltpu.CompilerParams(dimension_semantics=sem, vmem_limit_bytes=VMEM_LIMIT)


def _full(shape):
    return pl.BlockSpec(shape, lambda *_: (0,) * len(shape))


def _conv_mixer_kernel(x_ref, g_ref, win_ref, cw_ref, wout_ref, o_ref, carry_ref):
    d = x_ref.shape[1]
    tm = x_ref.shape[0]

    @pl.when(pl.program_id(0) == 0)
    def _():
        carry_ref[...] = jnp.zeros_like(carry_ref)

    x = x_ref[...]
    hn = _rms(x, g_ref[...]).astype(BF16)
    bcu = jnp.dot(hn, win_ref[...], preferred_element_type=F32)
    b, c, u = bcu[:, :d], bcu[:, d:2 * d], bcu[:, 2 * d:]
    z = c * u
    row = lax.broadcasted_iota(jnp.int32, z.shape, 0)
    prev1 = carry_ref[7:8, :]
    prev2 = carry_ref[6:7, :]
    z1 = jnp.where(row == 0, prev1, pltpu.roll(z, 1, axis=0))
    z2 = jnp.where(row == 0, prev2, jnp.where(row == 1, prev1, pltpu.roll(z, 2, axis=0)))
    conv = cw_ref[0:1, :] * z2 + cw_ref[1:2, :] * z1 + cw_ref[2:3, :] * z
    carry_ref[...] = z[tm - 8:, :]
    y = (b * conv).astype(BF16)
    o_ref[...] = x + jnp.dot(y, wout_ref[...], preferred_element_type=F32)


def _conv_mixer(h, g, w_in, conv_w, w_out):
    tp, d = h.shape
    tm = TOK_TILE
    return pl.pallas_call(
        _conv_mixer_kernel,
        grid=(tp // tm,),
        in_specs=[
            pl.BlockSpec((tm, d), lambda i: (i, 0)),
            _full((1, d)),
            _full((d, 3 * d)),
            _full((3, d)),
            _full((d, d)),
        ],
        out_specs=pl.BlockSpec((tm, d), lambda i: (i, 0)),
        out_shape=jax.ShapeDtypeStruct((tp, d), F32),
        scratch_shapes=[pltpu.VMEM((8, d), F32)],
        compiler_params=_cparams("arbitrary"),
        name="conv_mixer",
    )(h, g, w_in, conv_w, w_out)


def _pack_bf16_pair(x):
    n = x.shape[1] // 2
    hi = lax.bitcast_convert_type(x[:, :n].astype(BF16).astype(F32), jnp.uint32)
    lo = lax.bitcast_convert_type(x[:, n:].astype(BF16).astype(F32), jnp.uint32)
    return hi | (lo >> 16)


def _unpack_bf16_pair(p):
    hi = lax.bitcast_convert_type(p & jnp.uint32(0xFFFF0000), F32).astype(BF16)
    lo = lax.bitcast_convert_type(p << 16, F32).astype(BF16)
    return hi, lo


def _router_kernel(h_ref, g_ref, wrt_ref, hn_ref, info_ref, gate_ref, cnt_ref, carry_ref):
    tm = h_ref.shape[0]

    @pl.when(pl.program_id(0) == 0)
    def _():
        carry_ref[...] = jnp.zeros_like(carry_ref)

    hn = _rms(h_ref[...], g_ref[...])
    hn_ref[...] = _pack_bf16_pair(hn)
    logits = lax.dot_general(wrt_ref[...], hn, (((1,), (1,)), ((), ())),
                             precision=lax.Precision.HIGHEST,
                             preferred_element_type=F32)
    row = lax.broadcasted_iota(jnp.int32, logits.shape, 0).astype(F32)
    neg = jnp.float32(-jnp.inf)
    big = jnp.float32(2 * LANES)

    is_g = row < N_GROUPS
    gmax = jnp.max(jnp.where(is_g, logits, neg), axis=0, keepdims=True)
    grp = jnp.min(jnp.where(is_g & (logits == gmax), row, big), axis=0, keepdims=True)
    gsum = jnp.sum(jnp.where(is_g, jnp.exp(logits - gmax), 0.0), axis=0, keepdims=True)
    p_g = 1.0 / gsum

    lo = EXPERT_ROW0 + EXPERTS_PER_GROUP * grp
    in_grp = (row >= lo) & (row < lo + EXPERTS_PER_GROUP)
    emax = jnp.max(jnp.where(in_grp, logits, neg), axis=0, keepdims=True)
    ex = jnp.where(in_grp, jnp.exp(logits - emax), 0.0)
    p = ex / jnp.sum(ex, axis=0, keepdims=True)
    pm1 = jnp.where(in_grp, p, -1.0)
    p1 = jnp.max(pm1, axis=0, keepdims=True)
    i1 = jnp.min(jnp.where(pm1 == p1, row, big), axis=0, keepdims=True)
    pm2 = jnp.where(row == i1, -1.0, pm1)
    p2 = jnp.max(pm2, axis=0, keepdims=True)
    i2 = jnp.min(jnp.where(pm2 == p2, row, big), axis=0, keepdims=True)
    denom = p1 + p2
    g1 = p_g * p1 / denom
    g2 = p_g * p2 / denom

    sel1 = row == i1
    sel2 = row == i2
    onehot = jnp.where(sel1 | sel2, 1.0, 0.0)
    s_i = lax.broadcasted_iota(jnp.int32, (tm, tm), 0)
    t_i = lax.broadcasted_iota(jnp.int32, (tm, tm), 1)
    earlier = jnp.where(s_i < t_i, 1.0, 0.0).astype(BF16)
    before = carry_ref[...] + jnp.dot(onehot.astype(BF16), earlier, preferred_element_type=F32)
    r1 = jnp.sum(jnp.where(sel1, before, 0.0), axis=0, keepdims=True)
    r2 = jnp.sum(jnp.where(sel2, before, 0.0), axis=0, keepdims=True)
    total = carry_ref[...] + jnp.sum(onehot, axis=1, keepdims=True)
    carry_ref[...] = total
    cnt_ref[...] = total

    info_ref[...] = jnp.zeros_like(info_ref)
    info_ref[0:1, :] = (i1 - EXPERT_ROW0).astype(jnp.int32)
    info_ref[1:2, :] = (i2 - EXPERT_ROW0).astype(jnp.int32)
    info_ref[2:3, :] = r1.astype(jnp.int32)
    info_ref[3:4, :] = r2.astype(jnp.int32)
    gate_rows = jnp.where(row == 0, g1, jnp.where(row == 1, g2, 0.0))
    gate_ref[...] = gate_rows.T


def _router(h, g, w_router_t):
    tp, d = h.shape
    tm = TOK_TILE
    row = lambda i: (i, 0)
    return pl.pallas_call(
        _router_kernel,
        grid=(tp // tm,),
        in_specs=[pl.BlockSpec((tm, d), row), _full((1, d)), _full((LANES, d))],
        out_specs=[
            pl.BlockSpec((tm, d // 2), row),
            pl.BlockSpec((8, tm), lambda i: (0, i)),
            pl.BlockSpec((tm, LANES), row),
            _full((LANES, 1)),
        ],
        out_shape=[
            jax.ShapeDtypeStruct((tp, d // 2), jnp.uint32),
            jax.ShapeDtypeStruct((8, tp), jnp.int32),
            jax.ShapeDtypeStruct((tp, LANES), F32),
            jax.ShapeDtypeStruct((LANES, 1), F32),
        ],
        scratch_shapes=[pltpu.VMEM((LANES, 1), F32)],
        compiler_params=_cparams("arbitrary"),
        name="moe_router",
    )(h, g, w_router_t)


def _row_copy(src, s, dst, t, sem):
    return pltpu.make_async_copy(src.at[pl.ds(s, 1), :], dst.at[pl.ds(t, 1), :], sem)


def _dispatch_kernel(dest_ref, hn_ref, xs_in_ref, xs_ref, sem):
    del xs_in_ref
    tm = hn_ref.shape[0]
    n_tok = dest_ref.shape[0] // 2
    base = pl.program_id(0) * tm

    def issue(r, carry):
        _row_copy(hn_ref, r, xs_ref, dest_ref[base + r], sem.at[0]).start()
        _row_copy(hn_ref, r, xs_ref, dest_ref[n_tok + base + r], sem.at[1]).start()
        return carry

    lax.fori_loop(0, tm, issue, 0, unroll=8)

    def drain(r, carry):
        _row_copy(hn_ref, 0, xs_ref, 0, sem.at[0]).wait()
        _row_copy(hn_ref, 0, xs_ref, 0, sem.at[1]).wait()
        return carry

    lax.fori_loop(0, tm, drain, 0, unroll=8)


def _dispatch(dest_flat, hn, n_slots):
    tp, d = hn.shape
    tm = TOK_TILE
    xs_init = jnp.zeros((n_slots, d), hn.dtype)
    any_spec = pl.BlockSpec(memory_space=pl.ANY)
    return pl.pallas_call(
        _dispatch_kernel,
        grid_spec=pltpu.PrefetchScalarGridSpec(
            num_scalar_prefetch=1,
            grid=(tp // tm,),
            in_specs=[pl.BlockSpec((tm, d), lambda i, dest: (i, 0)), any_spec],
            out_specs=any_spec,
            scratch_shapes=[pltpu.SemaphoreType.DMA((2,))],
        ),
        out_shape=jax.ShapeDtypeStruct((n_slots, d), hn.dtype),
        input_output_aliases={2: 0},
        compiler_params=_cparams("arbitrary"),
        name="moe_dispatch",
    )(dest_flat, hn, xs_init)


def _ffn_kernel(blk_e_ref, nact_ref, xs_ref, wg_ref, wu_ref, wd_ref, ys_ref):
    del blk_e_ref

    @pl.when(pl.program_id(0) < nact_ref[0])
    def _():
        x = jnp.concatenate(_unpack_bf16_pair(xs_ref[...]), axis=1)
        gate = jnp.dot(x, wg_ref[...].astype(BF16), preferred_element_type=F32)
        up = jnp.dot(x, wu_ref[...].astype(BF16), preferred_element_type=F32)
        hh = (gate * jax.nn.sigmoid(gate) * up).astype(BF16)
        ys_ref[...] = jnp.dot(hh, wd_ref[...].astype(BF16), preferred_element_type=F32)

    @pl.when(pl.program_id(0) >= nact_ref[0])
    def _():
        ys_ref[...] = jnp.zeros_like(ys_ref)


def _ffn(blk_expert, n_active, xs, w_gate, w_up, w_down, layer):
    n_slots = xs.shape[0]
    d, de = w_gate.shape[-2:]
    bm = FFN_BLOCK

    def rows(b, be, na):
        return (jnp.maximum(jnp.minimum(b, na[0] - 1), 0), 0)

    def w_idx(b, be, na):
        return (layer, be[b], 0, 0)

    return pl.pallas_call(
        _ffn_kernel,
        grid_spec=pltpu.PrefetchScalarGridSpec(
            num_scalar_prefetch=2,
            grid=(n_slots // bm,),
            in_specs=[
                pl.BlockSpec((bm, d // 2), rows),
                pl.BlockSpec((None, None, d, de), w_idx),
                pl.BlockSpec((None, None, d, de), w_idx),
                pl.BlockSpec((None, None, de, d), w_idx),
            ],
            out_specs=pl.BlockSpec((bm, d), lambda b, be, na: (b, 0)),
        ),
        out_shape=jax.ShapeDtypeStruct((n_slots, d), F32),
        compiler_params=_cparams("arbitrary"),
        name="moe_ffn",
    )(blk_expert, n_active, xs, w_gate, w_up, w_down)


def _combine_kernel(dest_ref, h_ref, gate_ref, ys_ref, *rest, final_norm):
    if final_norm:
        fg_ref, o_ref, buf_a, buf_b, sem = rest
    else:
        o_ref, buf_a, buf_b, sem = rest
    tm = h_ref.shape[0]
    n_tok = dest_ref.shape[0] // 2
    base = pl.program_id(0) * tm

    def issue(r, carry):
        _row_copy(ys_ref, dest_ref[base + r], buf_a, r, sem.at[0]).start()
        _row_copy(ys_ref, dest_ref[n_tok + base + r], buf_b, r, sem.at[1]).start()
        return carry

    lax.fori_loop(0, tm, issue, 0, unroll=8)

    def drain(r, carry):
        _row_copy(ys_ref, 0, buf_a, 0, sem.at[0]).wait()
        _row_copy(ys_ref, 0, buf_b, 0, sem.at[1]).wait()
        return carry

    lax.fori_loop(0, tm, drain, 0, unroll=8)

    gates = gate_ref[...]
    out = h_ref[...] + gates[:, 0:1] * buf_a[...] + gates[:, 1:2] * buf_b[...]
    if final_norm:
        out = _rms(out, fg_ref[...])
    o_ref[...] = out


def _combine(dest_flat, h, gates, ys, final_g):
    tp, d = h.shape
    tm = TOK_TILE
    row = lambda i, dest: (i, 0)
    in_specs = [
        pl.BlockSpec((tm, d), row),
        pl.BlockSpec((tm, LANES), row),
        pl.BlockSpec(memory_space=pl.ANY),
    ]
    args = [dest_flat, h, gates, ys]
    if final_g is not None:
        in_specs.append(pl.BlockSpec((1, d), lambda i, dest: (0, 0)))
        args.append(final_g)
    return pl.pallas_call(
        functools.partial(_combine_kernel, final_norm=final_g is not None),
        grid_spec=pltpu.PrefetchScalarGridSpec(
            num_scalar_prefetch=1,
            grid=(tp // tm,),
            in_specs=in_specs,
            out_specs=pl.BlockSpec((tm, d), row),
            scratch_shapes=[
                pltpu.VMEM((tm, d), F32),
                pltpu.VMEM((tm, d), F32),
                pltpu.SemaphoreType.DMA((2,)),
            ],
        ),
        out_shape=jax.ShapeDtypeStruct((tp, d), F32),
        compiler_params=_cparams("arbitrary"),
        name="moe_combine",
    )(*args)


def _hier_moe(h, norm_g, w_grp, w_exp, w_gate, w_up, w_down, layer, final_g=None):
    tp, d = h.shape
    w_router_t = jnp.zeros((LANES, d), F32)
    w_router_t = w_router_t.at[:N_GROUPS].set(w_grp[layer].T).at[EXPERT_ROW0:].set(w_exp[layer].T)
    hn, info, gates, cnt = _router(h, norm_g, w_router_t)

    bm = FFN_BLOCK
    n_blocks = -(-(2 * tp + N_EXPERTS * (bm - 1)) // bm)
    counts = cnt[EXPERT_ROW0:, 0].astype(jnp.int32)
    padded = ((counts + bm - 1) // bm) * bm
    pad_end = jnp.cumsum(padded)
    pad_start = pad_end - padded
    dest = pad_start[info[0:2]] + info[2:4]
    dest_flat = dest.reshape(-1).astype(jnp.int32)
    n_active = (pad_end[-1] // bm).astype(jnp.int32)
    blk = jnp.clip(jnp.arange(n_blocks, dtype=jnp.int32), 0, jnp.maximum(n_active - 1, 0))
    blk_expert = jnp.sum((pad_end[None, :] <= (blk * bm)[:, None]).astype(jnp.int32), axis=1)
    blk_expert = jnp.minimum(blk_expert, N_EXPERTS - 1).astype(jnp.int32)

    xs = _dispatch(dest_flat, hn, n_blocks * bm)
    ys = _ffn(blk_expert, n_active.reshape(1), xs, w_gate, w_up, w_down, layer)
    return _combine(dest_flat, h, gates, ys, final_g)


def _kv_proj_kernel(h_ref, g_ref, wd_ref, cg_ref, wu_ref, cos_ref, sin_ref, k_ref, vt_ref):
    hn = _rms(h_ref[...], g_ref[...]).astype(BF16)
    a = jnp.dot(hn, wd_ref[...], preferred_element_type=F32)
    c_kv = _rms(a[:, :KV_RANK], cg_ref[...]).astype(BF16)
    k_rope = (a[:, KV_RANK:KV_RANK + LANES] * cos_ref[...]
              + a[:, KV_RANK + LANES:] * sin_ref[...]).astype(BF16)
    kv = jnp.dot(c_kv, wu_ref[...], preferred_element_type=F32)
    per_head = QK_NOPE_DIM + V_DIM
    for hd in range(N_HEADS):
        k_ref[hd, :, :QK_NOPE_DIM] = kv[:, hd * per_head:hd * per_head + QK_NOPE_DIM].astype(BF16)
        k_ref[hd, :, QK_NOPE_DIM:] = k_rope
        vt_ref[hd] = kv[:, hd * per_head + QK_NOPE_DIM:(hd + 1) * per_head].T.astype(BF16)


def _kv_proj(h, g, w_dkv_ext, ckv_g, w_ukv, cos_t, sin_t):
    tp, d = h.shape
    tm = TOK_TILE
    row = lambda i: (i, 0)
    return pl.pallas_call(
        _kv_proj_kernel,
        grid=(tp // tm,),
        in_specs=[
            pl.BlockSpec((tm, d), row),
            _full((1, d)),
            _full(w_dkv_ext.shape),
            _full((1, KV_RANK)),
            _full(w_ukv.shape),
            pl.BlockSpec((tm, LANES), row),
            pl.BlockSpec((tm, LANES), row),
        ],
        out_specs=[
            pl.BlockSpec((N_HEADS, tm, HEAD_PAD), lambda i: (0, i, 0)),
            pl.BlockSpec((N_HEADS, V_DIM, tm), lambda i: (0, 0, i)),
        ],
        out_shape=[
            jax.ShapeDtypeStruct((N_HEADS, tp, HEAD_PAD), BF16),
            jax.ShapeDtypeStruct((N_HEADS, V_DIM, tp), BF16),
        ],
        compiler_params=_cparams("parallel"),
        name="kv_proj",
    )(h, g, w_dkv_ext, ckv_g, w_ukv, cos_t, sin_t)


def _q_proj_kernel(h_ref, g_ref, wd_ref, cg_ref, wu_ref, cos_ref, sin_ref, q_ref):
    hn = _rms(h_ref[...], g_ref[...]).astype(BF16)
    cq = _rms(jnp.dot(hn, wd_ref[...], preferred_element_type=F32), cg_ref[...]).astype(BF16)
    q = jnp.dot(cq, wu_ref[...], preferred_element_type=F32)
    cos_t = cos_ref[...]
    sin_t = sin_ref[...]
    swapped0 = N_HEADS * HEAD_PAD
    for hd in range(N_HEADS):
        lo = hd * HEAD_PAD
        q_ref[hd, :, :QK_NOPE_DIM] = q[:, lo:lo + QK_NOPE_DIM].astype(BF16)
        rope = (q[:, lo + QK_NOPE_DIM:lo + HEAD_PAD] * cos_t
                + q[:, swapped0 + hd * LANES:swapped0 + (hd + 1) * LANES] * sin_t)
        q_ref[hd, :, QK_NOPE_DIM:] = rope.astype(BF16)


def _q_proj(h, g, w_dq, cq_g, w_uq_ext, cos_t, sin_t):
    tp, d = h.shape
    tm = TOK_TILE
    row = lambda i: (i, 0)
    return pl.pallas_call(
        _q_proj_kernel,
        grid=(tp // tm,),
        in_specs=[
            pl.BlockSpec((tm, d), row),
            _full((1, d)),
            _full(w_dq.shape),
            _full((1, w_dq.shape[1])),
            _full(w_uq_ext.shape),
            pl.BlockSpec((tm, LANES), row),
            pl.BlockSpec((tm, LANES), row),
        ],
        out_specs=pl.BlockSpec((N_HEADS, tm, HEAD_PAD), lambda i: (0, i, 0)),
        out_shape=jax.ShapeDtypeStruct((N_HEADS, tp, HEAD_PAD), BF16),
        compiler_params=_cparams("parallel"),
        name="q_proj",
    )(h, g, w_dq, cq_g, w_uq_ext, cos_t, sin_t)


def _attn_kernel(qi_ref, kj_ref, q_ref, k_ref, vt_ref, o_ref, m_ref, l_ref, acc_ref):
    n_h, bq, _ = q_ref.shape
    bk = k_ref.shape[1]
    dv = vt_ref.shape[1]
    step_id = pl.program_id(1)
    qi = qi_ref[step_id]
    kj = kj_ref[step_id]
    q_lo = qi * bq
    k_lo = kj * bk

    @pl.when(kj == 0)
    def _():
        m_ref[...] = jnp.full_like(m_ref, -jnp.inf)
        l_ref[...] = jnp.zeros_like(l_ref)
        acc_ref[...] = jnp.zeros_like(acc_ref)

    def step(masked):
        for hd in range(n_h):
            st = lax.dot_general(k_ref[hd], q_ref[hd], (((1,), (1,)), ((), ())),
                                 preferred_element_type=F32)
            if masked:
                kpos = k_lo + lax.broadcasted_iota(jnp.int32, st.shape, 0)
                qpos = q_lo + lax.broadcasted_iota(jnp.int32, st.shape, 1)
                st = jnp.where(kpos <= qpos, st, -jnp.inf)
            m_old = m_ref[hd]
            m_new = jnp.maximum(m_old, jnp.max(st, axis=0, keepdims=True))
            alpha = jnp.exp2(m_old - m_new)
            pt = jnp.exp2(st - m_new)
            l_ref[hd] = alpha * l_ref[hd] + jnp.sum(pt, axis=0, keepdims=True)
            acc_ref[hd] = alpha * acc_ref[hd] + jnp.dot(vt_ref[hd], pt.astype(BF16),
                                                        preferred_element_type=F32)
            m_ref[hd] = m_new

    fully_visible = k_lo + bk - 1 <= q_lo

    @pl.when(fully_visible)
    def _():
        step(False)

    @pl.when(jnp.logical_not(fully_visible))
    def _():
        step(True)

    @pl.when(kj == (q_lo + bq - 1) // bk)
    def _():
        for hd in range(n_h):
            o_ref[:, hd * dv:(hd + 1) * dv] = (acc_ref[hd] / l_ref[hd]).T.astype(o_ref.dtype)


def _attention(q, k, vt, bq, bk):
    n_heads, tp, dq = q.shape
    dv = vt.shape[1]
    hps = HEADS_PER_STEP
    pairs = [(qi, kj) for qi in range(tp // bq) for kj in range((qi * bq + bq - 1) // bk + 1)]
    qi_tab = jnp.array([pq for pq, _ in pairs], jnp.int32)
    kj_tab = jnp.array([pk for _, pk in pairs], jnp.int32)

    return pl.pallas_call(
        _attn_kernel,
        grid_spec=pltpu.PrefetchScalarGridSpec(
            num_scalar_prefetch=2,
            grid=(n_heads // hps, len(pairs)),
            in_specs=[
                pl.BlockSpec((hps, bq, dq), lambda hp, st, qt, kt: (hp, qt[st], 0)),
                pl.BlockSpec((hps, bk, dq), lambda hp, st, qt, kt: (hp, kt[st], 0)),
                pl.BlockSpec((hps, dv, bk), lambda hp, st, qt, kt: (hp, 0, kt[st])),
            ],
            out_specs=pl.BlockSpec((bq, hps * dv), lambda hp, st, qt, kt: (qt[st], hp)),
            scratch_shapes=[
                pltpu.VMEM((hps, 1, bq), F32),
                pltpu.VMEM((hps, 1, bq), F32),
                pltpu.VMEM((hps, dv, bq), F32),
            ],
        ),
        out_shape=jax.ShapeDtypeStruct((tp, n_heads * dv), BF16),
        compiler_params=_cparams("parallel", "arbitrary"),
        name="mla_attention",
    )(qi_tab, kj_tab, q, k, vt)


def _out_proj_kernel(o_ref, w_ref, h_ref, out_ref):
    out_ref[...] = h_ref[...] + jnp.dot(o_ref[...], w_ref[...], preferred_element_type=F32)


def _out_proj(o, w_o, h):
    tp, d = h.shape
    tm = TOK_TILE
    row = lambda i: (i, 0)
    return pl.pallas_call(
        _out_proj_kernel,
        grid=(tp // tm,),
        in_specs=[pl.BlockSpec((tm, o.shape[1]), row), _full(w_o.shape), pl.BlockSpec((tm, d), row)],
        out_specs=pl.BlockSpec((tm, d), row),
        out_shape=jax.ShapeDtypeStruct((tp, d), F32),
        compiler_params=_cparams("parallel"),
        name="attn_out_proj",
    )(o, w_o, h)


def _rope_tables(tp):
    inv = ROPE_THETA ** (-jnp.arange(0, QK_ROPE_DIM, 2, dtype=F32) / QK_ROPE_DIM)
    ang = jnp.arange(tp, dtype=F32)[:, None] * inv[None, :]
    cos, sin = jnp.cos(ang), jnp.sin(ang)
    zeros = jnp.zeros((tp, LANES - QK_ROPE_DIM), F32)
    return (jnp.concatenate([cos, cos, zeros], axis=1),
            jnp.concatenate([-sin, sin, zeros], axis=1))


def _swap_halves(w):
    half = w.shape[-1] // 2
    return jnp.concatenate([w[..., half:], w[..., :half]], axis=-1)


def _attn_blocks(tp):
    blk = next(b for b in (ATTN_BLOCK, 512, 256) if tp % b == 0)
    return blk, blk


def kernel(x, meta_tokens, norm_mix_g, norm_ffn_g, conv_w_in, conv_w, conv_w_out, kv_norm_g, w_dkv,
           ckv_norm_g, w_ukv, w_dq, cq_norm_g, w_uq, w_o, w_grp, w_exp, w_gate, w_up, w_down,
           final_norm_g):
    bsz, seq, d = x.shape
    assert bsz == 1, "the token-major kernels assume a single sequence"
    assert norm_mix_g.shape[0] == 2, "one short-conv layer followed by one MLA layer"
    t = N_META + seq
    tp = -(-t // TOK_TILE) * TOK_TILE
    h = jnp.concatenate([meta_tokens.astype(x.dtype), x[0], jnp.zeros((tp - t, d), x.dtype)], axis=0)

    h = _conv_mixer(h, norm_mix_g[0][None], conv_w_in[0].astype(BF16), conv_w[0, :, 0, :],
                    conv_w_out[0].astype(BF16))
    h = _hier_moe(h, norm_ffn_g[0][None], w_grp, w_exp, w_gate, w_up, w_down, 0)

    cos_t, sin_t = _rope_tables(tp)
    zeros_r = jnp.zeros((d, LANES - QK_ROPE_DIM), F32)
    w_kr = w_dkv[:, KV_RANK:]
    w_dkv_ext = jnp.concatenate([w_dkv[:, :KV_RANK], w_kr, zeros_r, _swap_halves(w_kr), zeros_r],
                                axis=1).astype(BF16)
    k, vt = _kv_proj(h, kv_norm_g[None], w_dkv_ext, ckv_norm_g[None], w_ukv.astype(BF16), cos_t, sin_t)

    q_rank = w_dq.shape[-1]
    scale = (QK_NOPE_DIM + QK_ROPE_DIM) ** -0.5 * LOG2_E
    wq = w_uq[0].reshape(q_rank, N_HEADS, QK_NOPE_DIM + QK_ROPE_DIM) * scale
    w_qr = wq[:, :, QK_NOPE_DIM:]
    zeros_q = jnp.zeros((q_rank, N_HEADS, LANES - QK_ROPE_DIM), F32)
    w_uq_ext = jnp.concatenate([
        jnp.concatenate([wq[:, :, :QK_NOPE_DIM], w_qr, zeros_q], axis=-1).reshape(q_rank, -1),
        jnp.concatenate([_swap_halves(w_qr), zeros_q], axis=-1).reshape(q_rank, -1),
    ], axis=1).astype(BF16)
    q = _q_proj(h, norm_mix_g[1][None], w_dq[0].astype(BF16), cq_norm_g[0][None], w_uq_ext, cos_t, sin_t)
    o = _attention(q, k, vt, *_attn_blocks(tp))
    h = _out_proj(o, w_o[0].astype(BF16), h)
    h = _hier_moe(h, norm_ffn_g[1][None], w_grp, w_exp, w_gate, w_up, w_down, 1,
                  final_g=final_norm_g[None])
    return h[N_META:t][None]
```

```python
import functools

import jax
import jax.numpy as jnp
from jax import lax
from jax.experimental import pallas as pl
from jax.experimental.pallas import tpu as pltpu

F32 = jnp.float32
BF16 = jnp.bfloat16

N_META = 16
N_HEADS = 8
QK_NOPE_DIM = 128
QK_ROPE_DIM = 64
V_DIM = 128
KV_RANK = 256
ROPE_THETA = 10000.0
N_GROUPS = 8
EXPERTS_PER_GROUP = 8
N_EXPERTS = N_GROUPS * EXPERTS_PER_GROUP
NORM_EPS = 1e-6
LOG2_E = 1.4426950408889634

LANES = 128
HEAD_PAD = 256
TOK_TILE = 256
FFN_BLOCK = 256
EXPERT_ROW0 = 64
HEADS_PER_STEP = 2
ATTN_BLOCK = 1280
VMEM_LIMIT = 48 * 1024 * 1024


def _rms(x, g):
    return x * lax.rsqrt(jnp.mean(x * x, axis=-1, keepdims=True) + NORM_EPS) * g


def _cparams(*sem):
    return pltpu.CompilerParams(dimension_semantics=sem, vmem_limit_bytes=VMEM_LIMIT)


def _full(shape):
    return pl.BlockSpec(shape, lambda *_: (0,) * len(shape))


def _conv_mixer_kernel(x_ref, g_ref, win_ref, cw_ref, wout_ref, o_ref, carry_ref):
    d = x_ref.shape[1]
    tm = x_ref.shape[0]

    @pl.when(pl.program_id(0) == 0)
    def _():
        carry_ref[...] = jnp.zeros_like(carry_ref)

    x = x_ref[...]
    hn = _rms(x, g_ref[...]).astype(BF16)
    bcu = jnp.dot(hn, win_ref[...], preferred_element_type=F32)
    b, c, u = bcu[:, :d], bcu[:, d:2 * d], bcu[:, 2 * d:]
    z = c * u
    row = lax.broadcasted_iota(jnp.int32, z.shape, 0)
    prev1 = carry_ref[7:8, :]
    prev2 = carry_ref[6:7, :]
    z1 = jnp.where(row == 0, prev1, pltpu.roll(z, 1, axis=0))
    z2 = jnp.where(row == 0, prev2, jnp.where(row == 1, prev1, pltpu.roll(z, 2, axis=0)))
    conv = cw_ref[0:1, :] * z2 + cw_ref[1:2, :] * z1 + cw_ref[2:3, :] * z
    carry_ref[...] = z[tm - 8:, :]
    y = (b * conv).astype(BF16)
    o_ref[...] = x + jnp.dot(y, wout_ref[...], preferred_element_type=F32)


def _conv_mixer(h, g, w_in, conv_w, w_out):
    tp, d = h.shape
    tm = TOK_TILE
    return pl.pallas_call(
        _conv_mixer_kernel,
        grid=(tp // tm,),
        in_specs=[
            pl.BlockSpec((tm, d), lambda i: (i, 0)),
            _full((1, d)),
            _full((d, 3 * d)),
            _full((3, d)),
            _full((d, d)),
        ],
        out_specs=pl.BlockSpec((tm, d), lambda i: (i, 0)),
        out_shape=jax.ShapeDtypeStruct((tp, d), F32),
        scratch_shapes=[pltpu.VMEM((8, d), F32)],
        compiler_params=_cparams("arbitrary"),
        name="conv_mixer",
    )(h, g, w_in, conv_w, w_out)


def _pack_bf16_pair(x):
    n = x.shape[1] // 2
    hi = lax.bitcast_convert_type(x[:, :n].astype(BF16).astype(F32), jnp.uint32)
    lo = lax.bitcast_convert_type(x[:, n:].astype(BF16).astype(F32), jnp.uint32)
    return hi | (lo >> 16)


def _unpack_bf16_pair(p):
    hi = lax.bitcast_convert_type(p & jnp.uint32(0xFFFF0000), F32).astype(BF16)
    lo = lax.bitcast_convert_type(p << 16, F32).astype(BF16)
    return hi, lo


def _router_kernel(h_ref, g_ref, wrt_ref, hn_ref, info_ref, gate_ref, cnt_ref, carry_ref):
    tm = h_ref.shape[0]

    @pl.when(pl.program_id(0) == 0)
    def _():
        carry_ref[...] = jnp.zeros_like(carry_ref)

    hn = _rms(h_ref[...], g_ref[...])
    hn_ref[...] = _pack_bf16_pair(hn)
    logits = lax.dot_general(wrt_ref[...], hn, (((1,), (1,)), ((), ())),
                             precision=lax.Precision.HIGHEST,
                             preferred_element_type=F32)
    row = lax.broadcasted_iota(jnp.int32, logits.shape, 0).astype(F32)
    neg = jnp.float32(-jnp.inf)
    big = jnp.float32(2 * LANES)

    is_g = row < N_GROUPS
    gmax = jnp.max(jnp.where(is_g, logits, neg), axis=0, keepdims=True)
    grp = jnp.min(jnp.where(is_g & (logits == gmax), row, big), axis=0, keepdims=True)
    gsum = jnp.sum(jnp.where(is_g, jnp.exp(logits - gmax), 0.0), axis=0, keepdims=True)
    p_g = 1.0 / gsum

    lo = EXPERT_ROW0 + EXPERTS_PER_GROUP * grp
    in_grp = (row >= lo) & (row < lo + EXPERTS_PER_GROUP)
    emax = jnp.max(jnp.where(in_grp, logits, neg), axis=0, keepdims=True)
    ex = jnp.where(in_grp, jnp.exp(logits - emax), 0.0)
    p = ex / jnp.sum(ex, axis=0, keepdims=True)
    pm1 = jnp.where(in_grp, p, -1.0)
    p1 = jnp.max(pm1, axis=0, keepdims=True)
    i1 = jnp.min(jnp.where(pm1 == p1, row, big), axis=0, keepdims=True)
    pm2 = jnp.where(row == i1, -1.0, pm1)
    p2 = jnp.max(pm2, axis=0, keepdims=True)
    i2 = jnp.min(jnp.where(pm2 == p2, row, big), axis=0, keepdims=True)
    denom = p1 + p2
    g1 = p_g * p1 / denom
    g2 = p_g * p2 / denom

    sel1 = row == i1
    sel2 = row == i2
    onehot = jnp.where(sel1 | sel2, 1.0, 0.0)
    s_i = lax.broadcasted_iota(jnp.int32, (tm, tm), 0)
    t_i = lax.broadcasted_iota(jnp.int32, (tm, tm), 1)
    earlier = jnp.where(s_i < t_i, 1.0, 0.0).astype(BF16)
    before = carry_ref[...] + jnp.dot(onehot.astype(BF16), earlier, preferred_element_type=F32)
    r1 = jnp.sum(jnp.where(sel1, before, 0.0), axis=0, keepdims=True)
    r2 = jnp.sum(jnp.where(sel2, before, 0.0), axis=0, keepdims=True)
    total = carry_ref[...] + jnp.sum(onehot, axis=1, keepdims=True)
    carry_ref[...] = total
    cnt_ref[...] = total

    info_ref[...] = jnp.zeros_like(info_ref)
    info_ref[0:1, :] = (i1 - EXPERT_ROW0).astype(jnp.int32)
    info_ref[1:2, :] = (i2 - EXPERT_ROW0).astype(jnp.int32)
    info_ref[2:3, :] = r1.astype(jnp.int32)
    info_ref[3:4, :] = r2.astype(jnp.int32)
    gate_rows = jnp.where(row == 0, g1, jnp.where(row == 1, g2, 0.0))
    gate_ref[...] = gate_rows.T


def _router(h, g, w_router_t):
    tp, d = h.shape
    tm = TOK_TILE
    row = lambda i: (i, 0)
    return pl.pallas_call(
        _router_kernel,
        grid=(tp // tm,),
        in_specs=[pl.BlockSpec((tm, d), row), _full((1, d)), _full((LANES, d))],
        out_specs=[
            pl.BlockSpec((tm, d // 2), row),
            pl.BlockSpec((8, tm), lambda i: (0, i)),
            pl.BlockSpec((tm, LANES), row),
            _full((LANES, 1)),
        ],
        out_shape=[
            jax.ShapeDtypeStruct((tp, d // 2), jnp.uint32),
            jax.ShapeDtypeStruct((8, tp), jnp.int32),
            jax.ShapeDtypeStruct((tp, LANES), F32),
            jax.ShapeDtypeStruct((LANES, 1), F32),
        ],
        scratch_shapes=[pltpu.VMEM((LANES, 1), F32)],
        compiler_params=_cparams("arbitrary"),
        name="moe_router",
    )(h, g, w_router_t)


def _slots_kernel(info_ref, start_ref, dest_ref):
    n_tok = info_ref.shape[1]
    expert = lax.broadcasted_iota(jnp.int32, (N_EXPERTS, n_tok), 0)
    start = start_ref[...]
    for k in range(2):
        hit = expert == info_ref[k:k + 1, :]
        first = jnp.sum(jnp.where(hit, start, 0.0), axis=0, keepdims=True)
        dest_ref[k:k + 1, :] = first.astype(jnp.int32) + info_ref[2 + k:3 + k, :]


def _slots(info, pad_start):
    tp = info.shape[1]
    return pl.pallas_call(
        _slots_kernel,
        grid=(1,),
        in_specs=[_full(info.shape), _full((N_EXPERTS, 1))],
        out_specs=_full((2, tp)),
        out_shape=jax.ShapeDtypeStruct((2, tp), jnp.int32),
        compiler_params=_cparams("arbitrary"),
        name="moe_slots",
    )(info, pad_start.astype(F32)[:, None])


def _row_copy(src, s, dst, t, sem):
    return pltpu.make_async_copy(src.at[pl.ds(s, 1), :], dst.at[pl.ds(t, 1), :], sem)


def _dispatch_kernel(dest_ref, hn_ref, xs_in_ref, xs_ref, sem):
    del xs_in_ref
    tm = hn_ref.shape[0]
    n_tok = dest_ref.shape[0] // 2
    base = pl.program_id(0) * tm

    def issue(r, carry):
        _row_copy(hn_ref, r, xs_ref, dest_ref[base + r], sem.at[0]).start()
        _row_copy(hn_ref, r, xs_ref, dest_ref[n_tok + base + r], sem.at[1]).start()
        return carry

    lax.fori_loop(0, tm, issue, 0, unroll=8)

    for k in range(2):
        pltpu.make_async_copy(hn_ref, xs_ref.at[pl.ds(0, tm), :], sem.at[k]).wait()


def _dispatch(dest_flat, hn, n_slots):
    tp, d = hn.shape
    tm = TOK_TILE
    xs_init = jnp.zeros((n_slots, d), hn.dtype)
    any_spec = pl.BlockSpec(memory_space=pl.ANY)
    return pl.pallas_call(
        _dispatch_kernel,
        grid_spec=pltpu.PrefetchScalarGridSpec(
            num_scalar_prefetch=1,
            grid=(tp // tm,),
            in_specs=[pl.BlockSpec((tm, d), lambda i, dest: (i, 0)), any_spec],
            out_specs=any_spec,
            scratch_shapes=[pltpu.SemaphoreType.DMA((2,))],
        ),
        out_shape=jax.ShapeDtypeStruct((n_slots, d), hn.dtype),
        input_output_aliases={2: 0},
        compiler_params=_cparams("arbitrary"),
        name="moe_dispatch",
    )(dest_flat, hn, xs_init)


def _ffn_kernel(blk_e_ref, nact_ref, xs_ref, wg_ref, wu_ref, wd_ref, ys_ref):
    del blk_e_ref

    @pl.when(pl.program_id(0) < nact_ref[0])
    def _():
        x = jnp.concatenate(_unpack_bf16_pair(xs_ref[...]), axis=1)
        gate = jnp.dot(x, wg_ref[...].astype(BF16), preferred_element_type=F32)
        up = jnp.dot(x, wu_ref[...].astype(BF16), preferred_element_type=F32)
        hh = (gate * jax.nn.sigmoid(gate) * up).astype(BF16)
        ys_ref[...] = jnp.dot(hh, wd_ref[...].astype(BF16), preferred_element_type=F32)

    @pl.when(pl.program_id(0) >= nact_ref[0])
    def _():
        ys_ref[...] = jnp.zeros_like(ys_ref)


def _ffn(blk_expert, n_active, xs, w_gate, w_up, w_down, layer):
    n_slots = xs.shape[0]
    d, de = w_gate.shape[-2:]
    bm = FFN_BLOCK

    def rows(b, be, na):
        return (jnp.maximum(jnp.minimum(b, na[0] - 1), 0), 0)

    def w_idx(b, be, na):
        return (layer, be[b], 0, 0)

    return pl.pallas_call(
        _ffn_kernel,
        grid_spec=pltpu.PrefetchScalarGridSpec(
            num_scalar_prefetch=2,
            grid=(n_slots // bm,),
            in_specs=[
                pl.BlockSpec((bm, d // 2), rows),
                pl.BlockSpec((None, None, d, de), w_idx),
                pl.BlockSpec((None, None, d, de), w_idx),
                pl.BlockSpec((None, None, de, d), w_idx),
            ],
            out_specs=pl.BlockSpec((bm, d), lambda b, be, na: (b, 0)),
        ),
        out_shape=jax.ShapeDtypeStruct((n_slots, d), F32),
        compiler_params=_cparams("arbitrary"),
        name="moe_ffn",
    )(blk_expert, n_active, xs, w_gate, w_up, w_down)


def _combine_kernel(dest_ref, h_ref, gate_ref, ys_ref, *rest, final_norm):
    if final_norm:
        fg_ref, o_ref, buf_a, buf_b, sem = rest
    else:
        o_ref, buf_a, buf_b, sem = rest
    tm = h_ref.shape[0]
    n_tok = dest_ref.shape[0] // 2
    base = pl.program_id(0) * tm

    def issue(r, carry):
        _row_copy(ys_ref, dest_ref[base + r], buf_a, r, sem.at[0]).start()
        _row_copy(ys_ref, dest_ref[n_tok + base + r], buf_b, r, sem.at[1]).start()
        return carry

    lax.fori_loop(0, tm, issue, 0, unroll=8)

    pltpu.make_async_copy(ys_ref.at[pl.ds(0, tm), :], buf_a, sem.at[0]).wait()
    pltpu.make_async_copy(ys_ref.at[pl.ds(0, tm), :], buf_b, sem.at[1]).wait()

    gates = gate_ref[...]
    out = h_ref[...] + gates[:, 0:1] * buf_a[...] + gates[:, 1:2] * buf_b[...]
    if final_norm:
        out = _rms(out, fg_ref[...])
    o_ref[...] = out


def _combine(dest_flat, h, gates, ys, final_g):
    tp, d = h.shape
    tm = TOK_TILE
    row = lambda i, dest: (i, 0)
    in_specs = [
        pl.BlockSpec((tm, d), row),
        pl.BlockSpec((tm, LANES), row),
        pl.BlockSpec(memory_space=pl.ANY),
    ]
    args = [dest_flat, h, gates, ys]
    if final_g is not None:
        in_specs.append(pl.BlockSpec((1, d), lambda i, dest: (0, 0)))
        args.append(final_g)
    return pl.pallas_call(
        functools.partial(_combine_kernel, final_norm=final_g is not None),
        grid_spec=pltpu.PrefetchScalarGridSpec(
            num_scalar_prefetch=1,
            grid=(tp // tm,),
            in_specs=in_specs,
            out_specs=pl.BlockSpec((tm, d), row),
            scratch_shapes=[
                pltpu.VMEM((tm, d), F32),
                pltpu.VMEM((tm, d), F32),
                pltpu.SemaphoreType.DMA((2,)),
            ],
        ),
        out_shape=jax.ShapeDtypeStruct((tp, d), F32),
        compiler_params=_cparams("arbitrary"),
        name="moe_combine",
    )(*args)


def _hier_moe(h, norm_g, w_grp, w_exp, w_gate, w_up, w_down, layer, final_g=None):
    tp, d = h.shape
    w_router_t = jnp.zeros((LANES, d), F32)
    w_router_t = w_router_t.at[:N_GROUPS].set(w_grp[layer].T).at[EXPERT_ROW0:].set(w_exp[layer].T)
    hn, info, gates, cnt = _router(h, norm_g, w_router_t)

    bm = FFN_BLOCK
    n_blocks = -(-(2 * tp + N_EXPERTS * (bm - 1)) // bm)
    counts = cnt[EXPERT_ROW0:, 0].astype(jnp.int32)
    padded = ((counts + bm - 1) // bm) * bm
    pad_end = jnp.cumsum(padded)
    pad_start = pad_end - padded
    dest_flat = _slots(info, pad_start).reshape(-1)
    n_active = (pad_end[-1] // bm).astype(jnp.int32)
    blk = jnp.clip(jnp.arange(n_blocks, dtype=jnp.int32), 0, jnp.maximum(n_active - 1, 0))
    blk_expert = jnp.sum((pad_end[None, :] <= (blk * bm)[:, None]).astype(jnp.int32), axis=1)
    blk_expert = jnp.minimum(blk_expert, N_EXPERTS - 1).astype(jnp.int32)

    xs = _dispatch(dest_flat, hn, n_blocks * bm)
    ys = _ffn(blk_expert, n_active.reshape(1), xs, w_gate, w_up, w_down, layer)
    return _combine(dest_flat, h, gates, ys, final_g)


def _kv_proj_kernel(h_ref, g_ref, wd_ref, cg_ref, wu_ref, cos_ref, sin_ref, k_ref, vt_ref):
    hn = _rms(h_ref[...], g_ref[...]).astype(BF16)
    a = jnp.dot(hn, wd_ref[...], preferred_element_type=F32)
    c_kv = _rms(a[:, :KV_RANK], cg_ref[...]).astype(BF16)
    k_rope = (a[:, KV_RANK:KV_RANK + LANES] * cos_ref[...]
              + a[:, KV_RANK + LANES:] * sin_ref[...]).astype(BF16)
    kv = jnp.dot(c_kv, wu_ref[...], preferred_element_type=F32)
    per_head = QK_NOPE_DIM + V_DIM
    for hd in range(N_HEADS):
        k_ref[hd, :, :QK_NOPE_DIM] = kv[:, hd * per_head:hd * per_head + QK_NOPE_DIM].astype(BF16)
        k_ref[hd, :, QK_NOPE_DIM:] = k_rope
        vt_ref[hd] = kv[:, hd * per_head + QK_NOPE_DIM:(hd + 1) * per_head].T.astype(BF16)


def _kv_proj(h, g, w_dkv_ext, ckv_g, w_ukv, cos_t, sin_t):
    tp, d = h.shape
    tm = TOK_TILE
    row = lambda i: (i, 0)
    return pl.pallas_call(
        _kv_proj_kernel,
        grid=(tp // tm,),
        in_specs=[
            pl.BlockSpec((tm, d), row),
            _full((1, d)),
            _full(w_dkv_ext.shape),
            _full((1, KV_RANK)),
            _full(w_ukv.shape),
            pl.BlockSpec((tm, LANES), row),
            pl.BlockSpec((tm, LANES), row),
        ],
        out_specs=[
            pl.BlockSpec((N_HEADS, tm, HEAD_PAD), lambda i: (0, i, 0)),
            pl.BlockSpec((N_HEADS, V_DIM, tm), lambda i: (0, 0, i)),
        ],
        out_shape=[
            jax.ShapeDtypeStruct((N_HEADS, tp, HEAD_PAD), BF16),
            jax.ShapeDtypeStruct((N_HEADS, V_DIM, tp), BF16),
        ],
        compiler_params=_cparams("parallel"),
        name="kv_proj",
    )(h, g, w_dkv_ext, ckv_g, w_ukv, cos_t, sin_t)


def _q_proj_kernel(h_ref, g_ref, wd_ref, cg_ref, wu_ref, cos_ref, sin_ref, q_ref):
    hn = _rms(h_ref[...], g_ref[...]).astype(BF16)
    cq = _rms(jnp.dot(hn, wd_ref[...], preferred_element_type=F32), cg_ref[...]).astype(BF16)
    q = jnp.dot(cq, wu_ref[...], preferred_element_type=F32)
    cos_t = cos_ref[...]
    sin_t = sin_ref[...]
    swapped0 = N_HEADS * HEAD_PAD
    for hd in range(N_HEADS):
        lo = hd * HEAD_PAD
        q_ref[hd, :, :QK_NOPE_DIM] = q[:, lo:lo + QK_NOPE_DIM].astype(BF16)
        rope = (q[:, lo + QK_NOPE_DIM:lo + HEAD_PAD] * cos_t
                + q[:, swapped0 + hd * LANES:swapped0 + (hd + 1) * LANES] * sin_t)
        q_ref[hd, :, QK_NOPE_DIM:] = rope.astype(BF16)


def _q_proj(h, g, w_dq, cq_g, w_uq_ext, cos_t, sin_t):
    tp, d = h.shape
    tm = TOK_TILE
    row = lambda i: (i, 0)
    return pl.pallas_call(
        _q_proj_kernel,
        grid=(tp // tm,),
        in_specs=[
            pl.BlockSpec((tm, d), row),
            _full((1, d)),
            _full(w_dq.shape),
            _full((1, w_dq.shape[1])),
            _full(w_uq_ext.shape),
            pl.BlockSpec((tm, LANES), row),
            pl.BlockSpec((tm, LANES), row),
        ],
        out_specs=pl.BlockSpec((N_HEADS, tm, HEAD_PAD), lambda i: (0, i, 0)),
        out_shape=jax.ShapeDtypeStruct((N_HEADS, tp, HEAD_PAD), BF16),
        compiler_params=_cparams("parallel"),
        name="q_proj",
    )(h, g, w_dq, cq_g, w_uq_ext, cos_t, sin_t)


def _attn_kernel(qi_ref, kj_ref, q_ref, k_ref, vt_ref, o_ref, m_ref, l_ref, acc_ref):
    n_h, bq, _ = q_ref.shape
    bk = k_ref.shape[1]
    dv = vt_ref.shape[1]
    step_id = pl.program_id(1)
    qi = qi_ref[step_id]
    kj = kj_ref[step_id]
    q_lo = qi * bq
    k_lo = kj * bk

    @pl.when(kj == 0)
    def _():
        m_ref[...] = jnp.full_like(m_ref, -jnp.inf)
        l_ref[...] = jnp.zeros_like(l_ref)
        acc_ref[...] = jnp.zeros_like(acc_ref)

    def step(masked):
        for hd in range(n_h):
            st = lax.dot_general(k_ref[hd], q_ref[hd], (((1,), (1,)), ((), ())),
                                 preferred_element_type=F32)
            if masked:
                kpos = k_lo + lax.broadcasted_iota(jnp.int32, st.shape, 0)
                qpos = q_lo + lax.broadcasted_iota(jnp.int32, st.shape, 1)
                st = jnp.where(kpos <= qpos, st, -jnp.inf)
            m_old = m_ref[hd]
            m_new = jnp.maximum(m_old, jnp.max(st, axis=0, keepdims=True))
            alpha = jnp.exp2(m_old - m_new)
            pt = jnp.exp2(st - m_new)
            l_ref[hd] = alpha * l_ref[hd] + jnp.sum(pt, axis=0, keepdims=True)
            acc_ref[hd] = alpha * acc_ref[hd] + jnp.dot(vt_ref[hd], pt.astype(BF16),
                                                        preferred_element_type=F32)
            m_ref[hd] = m_new

    fully_visible = k_lo + bk - 1 <= q_lo

    @pl.when(fully_visible)
    def _():
        step(False)

    @pl.when(jnp.logical_not(fully_visible))
    def _():
        step(True)

    @pl.when(kj == (q_lo + bq - 1) // bk)
    def _():
        for hd in range(n_h):
            o_ref[:, hd * dv:(hd + 1) * dv] = (acc_ref[hd] / l_ref[hd]).T.astype(o_ref.dtype)


def _attention(q, k, vt, bq, bk):
    n_heads, tp, dq = q.shape
    dv = vt.shape[1]
    hps = HEADS_PER_STEP
    pairs = [(qi, kj) for qi in range(tp // bq) for kj in range((qi * bq + bq - 1) // bk + 1)]
    qi_tab = jnp.array([pq for pq, _ in pairs], jnp.int32)
    kj_tab = jnp.array([pk for _, pk in pairs], jnp.int32)

    return pl.pallas_call(
        _attn_kernel,
        grid_spec=pltpu.PrefetchScalarGridSpec(
            num_scalar_prefetch=2,
            grid=(n_heads // hps, len(pairs)),
            in_specs=[
                pl.BlockSpec((hps, bq, dq), lambda hp, st, qt, kt: (hp, qt[st], 0)),
                pl.BlockSpec((hps, bk, dq), lambda hp, st, qt, kt: (hp, kt[st], 0)),
                pl.BlockSpec((hps, dv, bk), lambda hp, st, qt, kt: (hp, 0, kt[st])),
            ],
            out_specs=pl.BlockSpec((bq, hps * dv), lambda hp, st, qt, kt: (qt[st], hp)),
            scratch_shapes=[
                pltpu.VMEM((hps, 1, bq), F32),
                pltpu.VMEM((hps, 1, bq), F32),
                pltpu.VMEM((hps, dv, bq), F32),
            ],
        ),
        out_shape=jax.ShapeDtypeStruct((tp, n_heads * dv), BF16),
        compiler_params=_cparams("parallel", "arbitrary"),
        name="mla_attention",
    )(qi_tab, kj_tab, q, k, vt)


def _out_proj_kernel(o_ref, w_ref, h_ref, out_ref):
    out_ref[...] = h_ref[...] + jnp.dot(o_ref[...], w_ref[...], preferred_element_type=F32)


def _out_proj(o, w_o, h):
    tp, d = h.shape
    tm = TOK_TILE
    row = lambda i: (i, 0)
    return pl.pallas_call(
        _out_proj_kernel,
        grid=(tp // tm,),
        in_specs=[pl.BlockSpec((tm, o.shape[1]), row), _full(w_o.shape), pl.BlockSpec((tm, d), row)],
        out_specs=pl.BlockSpec((tm, d), row),
        out_shape=jax.ShapeDtypeStruct((tp, d), F32),
        compiler_params=_cparams("parallel"),
        name="attn_out_proj",
    )(o, w_o, h)


def _rope_tables(tp):
    inv = ROPE_THETA ** (-jnp.arange(0, QK_ROPE_DIM, 2, dtype=F32) / QK_ROPE_DIM)
    ang = jnp.arange(tp, dtype=F32)[:, None] * inv[None, :]
    cos, sin = jnp.cos(ang), jnp.sin(ang)
    zeros = jnp.zeros((tp, LANES - QK_ROPE_DIM), F32)
    return (jnp.concatenate([cos, cos, zeros], axis=1),
            jnp.concatenate([-sin, sin, zeros], axis=1))


def _swap_halves(w):
    half = w.shape[-1] // 2
    return jnp.concatenate([w[..., half:], w[..., :half]], axis=-1)


def _attn_blocks(tp):
    blk = next(b for b in (ATTN_BLOCK, 512, 256) if tp % b == 0)
    return blk, blk


def kernel(x, meta_tokens, norm_mix_g, norm_ffn_g, conv_w_in, conv_w, conv_w_out, kv_norm_g, w_dkv,
           ckv_norm_g, w_ukv, w_dq, cq_norm_g, w_uq, w_o, w_grp, w_exp, w_gate, w_up, w_down,
           final_norm_g):
    bsz, seq, d = x.shape
    assert bsz == 1, "the token-major kernels assume a single sequence"
    assert norm_mix_g.shape[0] == 2, "one short-conv layer followed by one MLA layer"
    t = N_META + seq
    tp = -(-t // TOK_TILE) * TOK_TILE
    h = jnp.concatenate([meta_tokens.astype(x.dtype), x[0], jnp.zeros((tp - t, d), x.dtype)], axis=0)

    h = _conv_mixer(h, norm_mix_g[0][None], conv_w_in[0].astype(BF16), conv_w[0, :, 0, :],
                    conv_w_out[0].astype(BF16))
    h = _hier_moe(h, norm_ffn_g[0][None], w_grp, w_exp, w_gate, w_up, w_down, 0)

    cos_t, sin_t = _rope_tables(tp)
    zeros_r = jnp.zeros((d, LANES - QK_ROPE_DIM), F32)
    w_kr = w_dkv[:, KV_RANK:]
    w_dkv_ext = jnp.concatenate([w_dkv[:, :KV_RANK], w_kr, zeros_r, _swap_halves(w_kr), zeros_r],
                                axis=1).astype(BF16)
    k, vt = _kv_proj(h, kv_norm_g[None], w_dkv_ext, ckv_norm_g[None], w_ukv.astype(BF16), cos_t, sin_t)

    q_rank = w_dq.shape[-1]
    scale = (QK_NOPE_DIM + QK_ROPE_DIM) ** -0.5 * LOG2_E
    wq = w_uq[0].reshape(q_rank, N_HEADS, QK_NOPE_DIM + QK_ROPE_DIM) * scale
    w_qr = wq[:, :, QK_NOPE_DIM:]
    zeros_q = jnp.zeros((q_rank, N_HEADS, LANES - QK_ROPE_DIM), F32)
    w_uq_ext = jnp.concatenate([
        jnp.concatenate([wq[:, :, :QK_NOPE_DIM], w_qr, zeros_q], axis=-1).reshape(q_rank, -1),
        jnp.concatenate([_swap_halves(w_qr), zeros_q], axis=-1).reshape(q_rank, -1),
    ], axis=1).astype(BF16)
    q = _q_proj(h, norm_mix_g[1][None], w_dq[0].astype(BF16), cq_norm_g[0][None], w_uq_ext, cos_t, sin_t)
    o = _attention(q, k, vt, *_attn_blocks(tp))
    h = _out_proj(o, w_o[0].astype(BF16), h)
    h = _hier_moe(h, norm_ffn_g[1][None], w_grp, w_exp, w_gate, w_up, w_down, 1,
                  final_g=final_norm_g[None])
    return h[N_META:t][None]
```
